```python
import math
import jax, jax.numpy as jnp
from jax import lax
import numpy as np

D_MODEL = 2048
BATCH = 2
SEQ = 16384
DEPTH = 1

HEAD_DIM = 128
DILATED_GROUPS = ((128, 1), (512, 4), (2048, 16))
N_GROUPS = len(DILATED_GROUPS)
HEADS_PER_GROUP = 4
N_ATTN_HEADS = N_GROUPS * HEADS_PER_GROUP
ATTN_WIDTH = N_ATTN_HEADS * HEAD_DIM
ATTN_OUT_WIDTH = HEADS_PER_GROUP * HEAD_DIM
CONV_DIM = D_MODEL
CONV_WIDTH = 3
D_FF = 5632
BLOCK = 128
RMS_EPS = 1e-6
NEG_INF = -1e30
ALIBI_SLOPES = tuple(2.0 ** (-8.0 * (i + 1) / N_ATTN_HEADS) for i in range(N_ATTN_HEADS))
IN_SIZES = (ATTN_WIDTH, ATTN_WIDTH, ATTN_WIDTH,
            CONV_DIM, CONV_DIM, CONV_DIM,
            D_MODEL, D_MODEL)
W_IN_COLS = sum(IN_SIZES)
SPLIT_POINTS = tuple(int(v) for v in np.cumsum(IN_SIZES)[:-1])

kernel_name = "hybrid_dilated_attn_shortconv_macaron"


def rms_norm(x, gain):
    xf = x.astype(jnp.float32)
    y = xf * lax.rsqrt(jnp.mean(xf * xf, axis=-1, keepdims=True) + RMS_EPS)
    return (y * gain.astype(jnp.float32)).astype(x.dtype)


def swiglu(x, w_gate, w_up, w_down):
    return (jax.nn.silu(x @ w_gate) * (x @ w_up)) @ w_down


def dilated_window_attention(q, k, v, slopes, window, dilation):
    b, s, h, dh = q.shape
    L = s // dilation
    w = window // dilation
    nb = -(-L // BLOCK)
    Lp = nb * BLOCK

    def to_sub(t):
        t = t.reshape(b, L, dilation, h, dh).transpose(0, 2, 3, 1, 4)
        return jnp.pad(t, ((0, 0), (0, 0), (0, 0), (0, Lp - L), (0, 0)))

    def band(t):
        t = jnp.pad(t, ((0, 0), (0, 0), (0, 0), (BLOCK, 0), (0, 0)))
        t = t.reshape(b, dilation, h, nb + 1, BLOCK, dh)
        return jnp.concatenate([t[:, :, :, :-1], t[:, :, :, 1:]], axis=4)

    qs, ks, vs = to_sub(q), to_sub(k), to_sub(v)
    qb = qs.reshape(b, dilation, h, nb, BLOCK, dh)
    kb, vb = band(ks), band(vs)

    scores = jnp.einsum('brhnqc,brhnkc->brhnqk', qb, kb).astype(jnp.float32)
    qi = jnp.arange(BLOCK)[:, None]
    ki = jnp.arange(2 * BLOCK)[None, :]
    dist = BLOCK + qi - ki
    key_pos = jnp.arange(nb)[:, None, None] * BLOCK + ki[None] - BLOCK
    valid = (dist >= 0) & (dist <= w) & (key_pos >= 0)
    slopes_arr = jnp.asarray(slopes, jnp.float32)
    alibi = -slopes_arr[:, None, None, None] * (dilation * dist).astype(jnp.float32)
    scores = jnp.where(valid, scores + alibi, NEG_INF)

    m = jnp.max(scores, axis=-1, keepdims=True)
    p = jnp.exp(scores - m)
    l = jnp.sum(p, axis=-1, keepdims=True)
    out = jnp.einsum('brhnqk,brhnkc->brhnqc', (p / l).astype(v.dtype), vb)
    lse = (m + jnp.log(l))[..., 0]

    out = out.reshape(b, dilation, h, Lp, dh)[:, :, :, :L]
    out = out.transpose(0, 3, 1, 2, 4).reshape(b, s, h, dh)
    lse = lse.reshape(b, dilation, h, Lp)[..., :L]
    lse = lse.transpose(0, 3, 1, 2).reshape(b, s, h)
    return out, lse


def causal_depthwise_conv(z, w):
    c = z.shape[-1]
    return lax.conv_general_dilated(
        z, w.reshape(CONV_WIDTH, 1, c).astype(z.dtype),
        window_strides=(1,), padding=((CONV_WIDTH - 1, 0),),
        dimension_numbers=('NWC', 'WIO', 'NWC'), feature_group_count=c)


def setup_inputs(seed: int = 0) -> dict:
    key = jax.random.key(seed)
    ks = jax.random.split(key, 20)
    f32 = jnp.float32

    def dense(k, fan_in, fan_out, scale=1.0):
        return jax.random.normal(k, (DEPTH, fan_in, fan_out), f32) * (scale * fan_in ** -0.5)

    def gain(k, shape):
        return 1.0 + 0.02 * jax.random.normal(k, (DEPTH,) + shape, f32)

    return {
        "x": jax.random.normal(ks[0], (BATCH, SEQ, D_MODEL), f32),
        "ffn1_norm": gain(ks[1], (D_MODEL,)),
        "ffn1_w_gate": dense(ks[2], D_MODEL, D_FF),
        "ffn1_w_up": dense(ks[3], D_MODEL, D_FF),
        "ffn1_w_down": dense(ks[4], D_FF, D_MODEL),
        "mix_norm": gain(ks[5], (D_MODEL,)),
        "w_in": dense(ks[6], D_MODEL, W_IN_COLS),
        "q_norm": gain(ks[7], (N_GROUPS, HEADS_PER_GROUP, HEAD_DIM)),
        "k_norm": gain(ks[8], (N_GROUPS, HEADS_PER_GROUP, HEAD_DIM)),
        "conv_w": jax.random.normal(ks[9], (DEPTH, CONV_WIDTH, CONV_DIM), f32) * CONV_WIDTH ** -0.5,
        "w_attn_out": dense(ks[10], ATTN_OUT_WIDTH, D_MODEL),
        "w_conv_out": dense(ks[11], CONV_DIM, D_MODEL),
        "w_o": dense(ks[12], D_MODEL, D_MODEL),
        "ffn2_norm": gain(ks[13], (D_MODEL,)),
        "ffn2_w_gate": dense(ks[14], D_MODEL, D_FF),
        "ffn2_w_up": dense(ks[15], D_MODEL, D_FF),
        "ffn2_w_down": dense(ks[16], D_FF, D_MODEL),
    }


def reference(x, ffn1_norm, ffn1_w_gate, ffn1_w_up, ffn1_w_down, mix_norm, w_in,
              q_norm, k_norm, conv_w, w_attn_out, w_conv_out, w_o,
              ffn2_norm, ffn2_w_gate, ffn2_w_up, ffn2_w_down):
    b, s, _ = x.shape
    for layer in range(DEPTH):
        x = x + 0.5 * swiglu(rms_norm(x, ffn1_norm[layer]),
                             ffn1_w_gate[layer], ffn1_w_up[layer], ffn1_w_down[layer])

        h = rms_norm(x, mix_norm[layer])
        proj = h @ w_in[layer]
        q, k, v, u, gate_b, gate_c, g_attn, g_conv = jnp.split(proj, SPLIT_POINTS, axis=-1)

        hshape = (b, s, N_GROUPS, HEADS_PER_GROUP, HEAD_DIM)
        q = rms_norm(q.reshape(hshape), q_norm[layer]) * (HEAD_DIM ** -0.5)
        k = rms_norm(k.reshape(hshape), k_norm[layer])
        v = v.reshape(hshape)
        outs, lses = [], []
        for g, (window, dilation) in enumerate(DILATED_GROUPS):
            slopes = ALIBI_SLOPES[g * HEADS_PER_GROUP:(g + 1) * HEADS_PER_GROUP]
            o_g, lse_g = dilated_window_attention(q[:, :, g], k[:, :, g], v[:, :, g],
                                                  slopes, window, dilation)
            outs.append(o_g)
            lses.append(lse_g)
        outs = jnp.stack(outs, axis=0)
        alpha = jax.nn.softmax(jnp.stack(lses, axis=0), axis=0)
        attn = jnp.sum(alpha[..., None].astype(outs.dtype) * outs, axis=0)
        branch_a = attn.reshape(b, s, ATTN_OUT_WIDTH) @ w_attn_out[layer]

        y = gate_b * causal_depthwise_conv(gate_c * u, conv_w[layer])
        branch_b = y @ w_conv_out[layer]

        merged = jax.nn.sigmoid(g_attn) * branch_a + jax.nn.sigmoid(g_conv) * branch_b
        x = x + merged @ w_o[layer]

        x = x + 0.5 * swiglu(rms_norm(x, ffn2_norm[layer]),
                             ffn2_w_gate[layer], ffn2_w_up[layer], ffn2_w_down[layer])
    return x
```

```python
import functools

import jax
import jax.numpy as jnp
from jax import lax
from jax.experimental import pallas as pl
from jax.experimental.pallas import tpu as pltpu

F32 = jnp.float32
BF16 = jnp.bfloat16

D_MODEL = 2048
D_FF = 5632
HEAD_DIM = 128
DILATED_GROUPS = ((128, 1), (512, 4), (2048, 16))
N_GROUPS = len(DILATED_GROUPS)
HEADS_PER_GROUP = 4
N_ATTN_HEADS = N_GROUPS * HEADS_PER_GROUP
ATTN_WIDTH = N_ATTN_HEADS * HEAD_DIM
GROUP_WIDTH = HEADS_PER_GROUP * HEAD_DIM
QKV_WIDTH = 3 * ATTN_WIDTH
CONV_WIDTH = 3
BLOCK = 128
RMS_EPS = 1e-6
NEG_INF = -1e30
ALIBI_SLOPES = tuple(2.0 ** (-8.0 * (i + 1) / N_ATTN_HEADS) for i in range(N_ATTN_HEADS))
U_OFF = QKV_WIDTH
GB_OFF = U_OFF + D_MODEL
GC_OFF = GB_OFF + D_MODEL
GA_OFF = GC_OFF + D_MODEL
GV_OFF = GA_OFF + D_MODEL

VMEM_LIMIT_BYTES = 56 * 1024 * 1024
SUBLANES = 8

FFN_TM, FFN_TF = 512, 512
QKV_TM, QKV_TN = 1024, 512
GATE_TM, GATE_TN = 1024, 256
MIX_TM = 256


def _params(sem):
    return pltpu.CompilerParams(dimension_semantics=sem, vmem_limit_bytes=VMEM_LIMIT_BYTES)


def _rms(x, gain):
    ms = jnp.mean(x * x, axis=-1, keepdims=True)
    return x * lax.rsqrt(ms + RMS_EPS) * gain


def _ffn_body(x_ref, gain_ref, wg_ref, wu_ref, wd_ref, *rest, n_ff, emit_next):
    if emit_next:
        next_gain_ref, o_ref, h_ref, xn_ref = rest
    else:
        o_ref, xn_ref = rest
    j = pl.program_id(1)

    @pl.when(j == 0)
    def _():
        xn_ref[...] = _rms(x_ref[...], gain_ref[...]).astype(BF16)

    xn = xn_ref[...]
    g = jnp.dot(xn, wg_ref[...], preferred_element_type=F32)
    u = jnp.dot(xn, wu_ref[...], preferred_element_type=F32)
    a = (g * jax.nn.sigmoid(g) * u).astype(BF16)
    d = jnp.dot(a, wd_ref[...], preferred_element_type=F32)

    @pl.when(j == 0)
    def _():
        o_ref[...] = d

    @pl.when(j > 0)
    def _():
        o_ref[...] += d

    @pl.when(j == n_ff - 1)
    def _():
        y = x_ref[...] + 0.5 * o_ref[...]
        o_ref[...] = y
        if emit_next:
            h_ref[...] = _rms(y, next_gain_ref[...]).astype(BF16)


def _ffn(x, gain, wg, wu, wd, next_gain=None):
    t, d = x.shape
    n_ff = D_FF // FFN_TF
    emit_next = next_gain is not None
    row = pl.BlockSpec((FFN_TM, d), lambda i, j: (i, 0))
    vec = pl.BlockSpec((1, d), lambda i, j: (0, 0))
    in_specs = [row, vec,
                pl.BlockSpec((d, FFN_TF), lambda i, j: (0, j)),
                pl.BlockSpec((d, FFN_TF), lambda i, j: (0, j)),
                pl.BlockSpec((FFN_TF, d), lambda i, j: (j, 0))]
    args = [x, gain.reshape(1, d), wg, wu, wd]
    out_shape = jax.ShapeDtypeStruct((t, d), F32)
    out_specs = row
    if emit_next:
        in_specs.append(vec)
        args.append(next_gain.reshape(1, d))
        out_shape = (out_shape, jax.ShapeDtypeStruct((t, d), BF16))
        out_specs = (row, row)
    return pl.pallas_call(
        functools.partial(_ffn_body, n_ff=n_ff, emit_next=emit_next),
        grid=(t // FFN_TM, n_ff),
        in_specs=in_specs,
        out_specs=out_specs,
        out_shape=out_shape,
        scratch_shapes=[pltpu.VMEM((FFN_TM, d), BF16)],
        compiler_params=_params(("parallel", "arbitrary")),
        name="ffn",
    )(*args)


def _qkv_body(h_ref, w_ref, gain_ref, o_ref, *, n_norm_blocks):
    j = pl.program_id(1)
    acc = jnp.dot(h_ref[...], w_ref[...], preferred_element_type=F32)

    @pl.when(j < n_norm_blocks)
    def _():
        for hh in range(QKV_TN // HEAD_DIM):
            sl = slice(hh * HEAD_DIM, (hh + 1) * HEAD_DIM)
            o_ref[:, sl] = _rms(acc[:, sl], gain_ref[:, sl]).astype(BF16)

    @pl.when(j >= n_norm_blocks)
    def _():
        o_ref[...] = acc.astype(BF16)


def _qkv_proj(h, w_in, qk_gain):
    t, d = h.shape
    return pl.pallas_call(
        functools.partial(_qkv_body, n_norm_blocks=2 * ATTN_WIDTH // QKV_TN),
        grid=(t // QKV_TM, QKV_WIDTH // QKV_TN),
        in_specs=[pl.BlockSpec((QKV_TM, d), lambda i, j: (i, 0)),
                  pl.BlockSpec((d, QKV_TN), lambda i, j: (0, j)),
                  pl.BlockSpec((1, QKV_TN), lambda i, j: (0, j))],
        out_specs=pl.BlockSpec((QKV_TM, QKV_TN), lambda i, j: (i, j)),
        out_shape=jax.ShapeDtypeStruct((t, QKV_WIDTH), BF16),
        compiler_params=_params(("parallel", "arbitrary")),
        name="qkv_proj",
    )(h, w_in, qk_gain)


def _gate_body(h_ref, wu_ref, wb_ref, wc_ref, wa_ref, wv_ref, z_ref, gb_ref, sa_ref, sv_ref):
    h = h_ref[...]
    dot = lambda w_ref: jnp.dot(h, w_ref[...], preferred_element_type=F32)
    z_ref[...] = (dot(wc_ref) * dot(wu_ref)).astype(BF16)
    gb_ref[...] = dot(wb_ref).astype(BF16)
    sa_ref[...] = jax.nn.sigmoid(dot(wa_ref)).astype(BF16)
    sv_ref[...] = jax.nn.sigmoid(dot(wv_ref)).astype(BF16)


def _gate_proj(h, w_in):
    t, d = h.shape
    wspec = lambda off: pl.BlockSpec((d, GATE_TN), lambda i, j: (0, off // GATE_TN + j))
    ospec = pl.BlockSpec((GATE_TM, GATE_TN), lambda i, j: (i, j))
    oshape = jax.ShapeDtypeStruct((t, d), BF16)
    return pl.pallas_call(
        _gate_body,
        grid=(t // GATE_TM, d // GATE_TN),
        in_specs=[pl.BlockSpec((GATE_TM, d), lambda i, j: (i, 0)),
                  wspec(U_OFF), wspec(GB_OFF), wspec(GC_OFF), wspec(GA_OFF), wspec(GV_OFF)],
        out_specs=(ospec,) * 4,
        out_shape=(oshape,) * 4,
        compiler_params=_params(("parallel", "arbitrary")),
        name="gate_proj",
    )(h, w_in, w_in, w_in, w_in, w_in)


def _attn_body(q_ref, kp_ref, kc_ref, vp_ref, vc_ref, o_ref, lse_ref, *, slopes, dilation):
    n = pl.program_id(2)
    qi = lax.broadcasted_iota(jnp.int32, (BLOCK, 2 * BLOCK), 0)
    ki = lax.broadcasted_iota(jnp.int32, (BLOCK, 2 * BLOCK), 1)
    dist = BLOCK + qi - ki
    valid = (dist >= 0) & (dist <= BLOCK) & ((ki >= BLOCK) | (n > 0))
    ddist = (dilation * dist).astype(F32)
    for hh in range(HEADS_PER_GROUP):
        sl = slice(hh * HEAD_DIM, (hh + 1) * HEAD_DIM)
        q = q_ref[0, :, sl]
        k = jnp.concatenate([kp_ref[0, :, sl], kc_ref[0, :, sl]], axis=0)
        v = jnp.concatenate([vp_ref[0, :, sl], vc_ref[0, :, sl]], axis=0)
        s = lax.dot_general(q, k, (((1,), (1,)), ((), ())), preferred_element_type=F32)
        s = jnp.where(valid, s - slopes[hh] * ddist, NEG_INF)
        m = jnp.max(s, axis=-1, keepdims=True)
        p = jnp.exp(s - m)
        l = jnp.sum(p, axis=-1, keepdims=True)
        o = jnp.dot(p.astype(BF16), v, preferred_element_type=F32) / l
        o_ref[0, :, sl] = o.astype(BF16)
        lse_ref[0, :, sl] = jnp.broadcast_to(m + jnp.log(l), (BLOCK, HEAD_DIM))


def _attention_group(qkv, g, batch, seq):
    window, d = DILATED_GROUPS[g]
    assert window // d == BLOCK and seq % (d * BLOCK) == 0
    sub_len = seq // d
    nb = sub_len // BLOCK
    per_row = QKV_WIDTH // GROUP_WIDTH
    view = qkv.reshape(batch, sub_len, d * QKV_WIDTH)

    def spec(part, prev):
        col = part * N_GROUPS + g
        if prev:
            return pl.BlockSpec((1, BLOCK, GROUP_WIDTH),
                                lambda b, r, n: (b, jnp.maximum(n - 1, 0), r * per_row + col))
        return pl.BlockSpec((1, BLOCK, GROUP_WIDTH), lambda b, r, n: (b, n, r * per_row + col))

    ospec = pl.BlockSpec((1, BLOCK, GROUP_WIDTH), lambda b, r, n: (b, n, r))
    slopes = ALIBI_SLOPES[g * HEADS_PER_GROUP:(g + 1) * HEADS_PER_GROUP]
    o, lse = pl.pallas_call(
        functools.partial(_attn_body, slopes=slopes, dilation=d),
        grid=(batch, d, nb),
        in_specs=[spec(0, False), spec(1, True), spec(1, False), spec(2, True), spec(2, False)],
        out_specs=(ospec, ospec),
        out_shape=(jax.ShapeDtypeStruct((batch, sub_len, d * GROUP_WIDTH), BF16),
                   jax.ShapeDtypeStruct((batch, sub_len, d * GROUP_WIDTH), F32)),
        compiler_params=_params(("parallel", "parallel", "arbitrary")),
        name=f"attention_g{g}",
    )(view, view, view, view, view)
    t = batch * seq
    return o.reshape(t, GROUP_WIDTH), lse.reshape(t, GROUP_WIDTH)


def _mix_body(o0_ref, o1_ref, o2_ref, l0_ref, l1_ref, l2_ref, z_ref, zh_ref, gb_ref, sa_ref, sv_ref,
              x_ref, cw_ref, wa_ref, wc_ref, wo_ref, out_ref, *, tiles_per_seq):
    i = pl.program_id(0)
    l0, l1, l2 = l0_ref[...], l1_ref[...], l2_ref[...]
    m = jnp.maximum(jnp.maximum(l0, l1), l2)
    e0, e1, e2 = jnp.exp(l0 - m), jnp.exp(l1 - m), jnp.exp(l2 - m)
    attn = (e0 * o0_ref[...].astype(F32) + e1 * o1_ref[...].astype(F32)
            + e2 * o2_ref[...].astype(F32)) / (e0 + e1 + e2)
    branch_a = jnp.dot(attn.astype(BF16), wa_ref[...], preferred_element_type=F32)

    z = z_ref[...].astype(F32)
    halo = zh_ref[...].astype(F32)
    halo = jnp.where(i % tiles_per_seq == 0, 0.0, halo)
    h1 = halo[SUBLANES - 1:SUBLANES, :]
    h2 = halo[SUBLANES - 2:SUBLANES - 1, :]
    rows = lax.broadcasted_iota(jnp.int32, z.shape, 0)
    zm1 = jnp.where(rows == 0, h1, pltpu.roll(z, 1, 0))
    zm2 = jnp.where(rows == 0, h2, jnp.where(rows == 1, h1, pltpu.roll(z, 2, 0)))
    conv = cw_ref[2:3, :] * z + cw_ref[1:2, :] * zm1 + cw_ref[0:1, :] * zm2
    y = gb_ref[...].astype(F32) * conv
    branch_b = jnp.dot(y.astype(BF16), wc_ref[...], preferred_element_type=F32)

    merged = sa_ref[...].astype(F32) * branch_a + sv_ref[...].astype(F32) * branch_b
    out_ref[...] = x_ref[...] + jnp.dot(merged.astype(BF16), wo_ref[...], preferred_element_type=F32)


def _mix(os, lses, z, gb, sa, sv, x, conv_w, wa, wc, wo, seq):
    t, d = x.shape
    assert seq % MIX_TM == 0
    row = lambda w: pl.BlockSpec((MIX_TM, w), lambda i: (i, 0))
    const = lambda a: pl.BlockSpec(a.shape, lambda i: (0, 0), pipeline_mode=pl.Buffered(1))
    halo = pl.BlockSpec((SUBLANES, d), lambda i: (jnp.maximum(i * (MIX_TM // SUBLANES) - 1, 0), 0))
    return pl.pallas_call(
        functools.partial(_mix_body, tiles_per_seq=seq // MIX_TM),
        grid=(t // MIX_TM,),
        in_specs=[row(GROUP_WIDTH)] * 6 + [row(d), halo, row(d), row(d), row(d), row(d),
                                           const(conv_w), const(wa), const(wc), const(wo)],
        out_specs=row(d),
        out_shape=jax.ShapeDtypeStruct((t, d), F32),
        compiler_params=_params(("parallel",)),
        name="mix",
    )(*os, *lses, z, z, gb, sa, sv, x, conv_w, wa, wc, wo)


def kernel(x, ffn1_norm, ffn1_w_gate, ffn1_w_up, ffn1_w_down, mix_norm, w_in, q_norm, k_norm, conv_w,
           w_attn_out, w_conv_out, w_o, ffn2_norm, ffn2_w_gate, ffn2_w_up, ffn2_w_down):
    batch, seq, d = x.shape
    t = batch * seq
    xt = x.reshape(t, d)
    for layer in range(ffn1_norm.shape[0]):
        bf = lambda w: w[layer].astype(BF16)
        xt, h = _ffn(xt, ffn1_norm[layer], bf(ffn1_w_gate), bf(ffn1_w_up), bf(ffn1_w_down),
                     next_gain=mix_norm[layer])
        w_in_b = bf(w_in)
        qk_gain = jnp.concatenate([q_norm[layer].reshape(-1) * (HEAD_DIM ** -0.5),
                                   k_norm[layer].reshape(-1),
                                   jnp.ones((ATTN_WIDTH,), F32)]).reshape(1, QKV_WIDTH)
        qkv = _qkv_proj(h, w_in_b, qk_gain)
        z, gb, sa, sv = _gate_proj(h, w_in_b)
        os, lses = zip(*[_attention_group(qkv, g, batch, seq) for g in range(N_GROUPS)])
        xt = _mix(os, lses, z, gb, sa, sv, xt, conv_w[layer], bf(w_attn_out), bf(w_conv_out), bf(w_o), seq)
        xt = _ffn(xt, ffn2_norm[layer], bf(ffn2_w_gate), bf(ffn2_w_up), bf(ffn2_w_down))
    return xt.reshape(batch, seq, d)
```

```python
import functools

import jax
import jax.numpy as jnp
from jax import lax
from jax.experimental import pallas as pl
from jax.experimental.pallas import tpu as pltpu

F32 = jnp.float32
BF16 = jnp.bfloat16

D_MODEL = 2048
D_FF = 5632
HEAD_DIM = 128
DILATED_GROUPS = ((128, 1), (512, 4), (2048, 16))
N_GROUPS = len(DILATED_GROUPS)
HEADS_PER_GROUP = 4
N_ATTN_HEADS = N_GROUPS * HEADS_PER_GROUP
ATTN_WIDTH = N_ATTN_HEADS * HEAD_DIM
GROUP_WIDTH = HEADS_PER_GROUP * HEAD_DIM
QKV_WIDTH = 3 * ATTN_WIDTH
CONV_WIDTH = 3
BLOCK = 128
RMS_EPS = 1e-6
NEG_INF = -1e30
ALIBI_SLOPES = tuple(2.0 ** (-8.0 * (i + 1) / N_ATTN_HEADS) for i in range(N_ATTN_HEADS))
U_OFF = QKV_WIDTH
GB_OFF = U_OFF + D_MODEL
GC_OFF = GB_OFF + D_MODEL
GA_OFF = GC_OFF + D_MODEL
GV_OFF = GA_OFF + D_MODEL

VMEM_LIMIT_BYTES = 56 * 1024 * 1024
SUBLANES = 8

FFN_TM, FFN_TF = 512, 512
QKV_TM, QKV_SUB = 1024, 256
GATE_TM, GATE_TN = 1024, 256
ATTN_CHUNK = 2048
MIX_TM = 256


def _params(sem):
    return pltpu.CompilerParams(dimension_semantics=sem, vmem_limit_bytes=VMEM_LIMIT_BYTES)


def _rms(x, gain):
    ms = jnp.mean(x * x, axis=-1, keepdims=True)
    return x * lax.rsqrt(ms + RMS_EPS) * gain


def _head(hh):
    return slice(hh * HEAD_DIM, (hh + 1) * HEAD_DIM)


def _ffn_body(x_ref, gain_ref, wg_ref, wu_ref, wd_ref, *rest, n_ff, emit_next):
    if emit_next:
        next_gain_ref, o_ref, h_ref, xn_ref = rest
    else:
        o_ref, xn_ref = rest
    j = pl.program_id(1)

    @pl.when(j == 0)
    def _():
        xn_ref[...] = _rms(x_ref[...], gain_ref[...]).astype(BF16)
        o_ref[...] = jnp.zeros_like(o_ref)

    xn = xn_ref[...]
    g = jnp.dot(xn, wg_ref[...], preferred_element_type=F32)
    u = jnp.dot(xn, wu_ref[...], preferred_element_type=F32)
    a = (g * jax.nn.sigmoid(g) * u).astype(BF16)
    o_ref[...] += jnp.dot(a, wd_ref[...], preferred_element_type=F32)

    @pl.when(j == n_ff - 1)
    def _():
        y = x_ref[...] + 0.5 * o_ref[...]
        o_ref[...] = y
        if emit_next:
            h_ref[...] = _rms(y, next_gain_ref[...]).astype(BF16)


def _ffn(x, gain, wg, wu, wd, next_gain=None):
    t, d = x.shape
    n_ff = D_FF // FFN_TF
    emit_next = next_gain is not None
    row = pl.BlockSpec((FFN_TM, d), lambda i, j: (i, 0))
    vec = pl.BlockSpec((1, d), lambda i, j: (0, 0))
    in_specs = [row, vec,
                pl.BlockSpec((d, FFN_TF), lambda i, j: (0, j)),
                pl.BlockSpec((d, FFN_TF), lambda i, j: (0, j)),
                pl.BlockSpec((FFN_TF, d), lambda i, j: (j, 0))]
    args = [x, gain.reshape(1, d), wg, wu, wd]
    out_shape = jax.ShapeDtypeStruct((t, d), F32)
    out_specs = row
    if emit_next:
        in_specs.append(vec)
        args.append(next_gain.reshape(1, d))
        out_shape = (out_shape, jax.ShapeDtypeStruct((t, d), BF16))
        out_specs = (row, row)
    return pl.pallas_call(
        functools.partial(_ffn_body, n_ff=n_ff, emit_next=emit_next),
        grid=(t // FFN_TM, n_ff),
        in_specs=in_specs,
        out_specs=out_specs,
        out_shape=out_shape,
        scratch_shapes=[pltpu.VMEM((FFN_TM, d), BF16)],
        compiler_params=_params(("parallel", "arbitrary")),
        name="ffn",
    )(*args)


def _qkv_body(h_ref, w_ref, gain_ref, o_ref, scr_ref, *, dilation):
    part = pl.program_id(1)
    w = w_ref[...]
    sub = QKV_SUB // dilation
    for ch in range(QKV_TM // QKV_SUB):
        tok = pl.ds(ch * QKV_SUB, QKV_SUB)
        acc = jnp.dot(h_ref[tok, :], w, preferred_element_type=F32)
        for hh in range(HEADS_PER_GROUP):
            blk = acc[:, _head(hh)]
            val = jnp.where(part < 2, _rms(blk, gain_ref[:, _head(hh)]), blk)
            if dilation == 1:
                o_ref[0, 0, 0, tok, _head(hh)] = val.astype(BF16)
            else:
                scr_ref[hh, tok, :] = val
        if dilation > 1:
            for r in range(dilation):
                for hh in range(HEADS_PER_GROUP):
                    src = scr_ref[hh, pl.ds(ch * QKV_SUB + r, sub, stride=dilation), :]
                    o_ref[0, 0, r, pl.ds(ch * sub, sub), _head(hh)] = src.astype(BF16)


def _qkv_proj(h, w_in, qk_gain, g, batch, seq):
    _, dil = DILATED_GROUPS[g]
    t, d = h.shape
    tiles_per_seq = seq // QKV_TM
    rows = QKV_TM // dil
    col = lambda i, p: (0, p * N_GROUPS + g)
    return pl.pallas_call(
        functools.partial(_qkv_body, dilation=dil),
        grid=(t // QKV_TM, 3),
        in_specs=[pl.BlockSpec((QKV_TM, d), lambda i, p: (i, 0)),
                  pl.BlockSpec((d, GROUP_WIDTH), col),
                  pl.BlockSpec((1, GROUP_WIDTH), col)],
        out_specs=pl.BlockSpec((1, 1, dil, rows, GROUP_WIDTH),
                               lambda i, p: (p, i // tiles_per_seq, 0, i % tiles_per_seq, 0)),
        out_shape=jax.ShapeDtypeStruct((3, batch, dil, seq // dil, GROUP_WIDTH), BF16),
        scratch_shapes=[pltpu.VMEM((HEADS_PER_GROUP, QKV_TM, HEAD_DIM), F32)],
        compiler_params=_params(("parallel", "arbitrary")),
        name=f"qkv_proj_g{g}",
    )(h, w_in, qk_gain)


def _gate_body(h_ref, wu_ref, wb_ref, wc_ref, wa_ref, wv_ref, z_ref, gb_ref, sa_ref, sv_ref):
    h = h_ref[...]
    dot = lambda w_ref: jnp.dot(h, w_ref[...], preferred_element_type=F32)
    z_ref[...] = (dot(wc_ref) * dot(wu_ref)).astype(BF16)
    gb_ref[...] = dot(wb_ref).astype(BF16)
    sa_ref[...] = jax.nn.sigmoid(dot(wa_ref)).astype(BF16)
    sv_ref[...] = jax.nn.sigmoid(dot(wv_ref)).astype(BF16)


def _gate_proj(h, w_in):
    t, d = h.shape
    wspec = lambda off: pl.BlockSpec((d, GATE_TN), lambda i, j: (0, off // GATE_TN + j))
    ospec = pl.BlockSpec((GATE_TM, GATE_TN), lambda i, j: (i, j))
    oshape = jax.ShapeDtypeStruct((t, d), BF16)
    return pl.pallas_call(
        _gate_body,
        grid=(t // GATE_TM, d // GATE_TN),
        in_specs=[pl.BlockSpec((GATE_TM, d), lambda i, j: (i, 0)),
                  wspec(U_OFF), wspec(GB_OFF), wspec(GC_OFF), wspec(GA_OFF), wspec(GV_OFF)],
        out_specs=(ospec,) * 4,
        out_shape=(oshape,) * 4,
        compiler_params=_params(("parallel", "arbitrary")),
        name="gate_proj",
    )(h, w_in, w_in, w_in, w_in, w_in)


def _attn_body(q_ref, kp_ref, kc_ref, vp_ref, vc_ref, o_ref, lse_ref, kbuf, vbuf, *, slopes, dilation, rows):
    c = pl.program_id(1)
    n_sub = rows // BLOCK
    kbuf[:, :BLOCK, :] = kp_ref[0, 0]
    kbuf[:, BLOCK:, :] = kc_ref[0, 0]
    vbuf[:, :BLOCK, :] = vp_ref[0, 0]
    vbuf[:, BLOCK:, :] = vc_ref[0, 0]

    qi = lax.broadcasted_iota(jnp.int32, (BLOCK, 2 * BLOCK), 0)
    ki = lax.broadcasted_iota(jnp.int32, (BLOCK, 2 * BLOCK), 1)
    dist = BLOCK + qi - ki
    band = (dist >= 0) & (dist <= BLOCK)
    in_cur = ki >= BLOCK
    ddist = (dilation * dist).astype(F32)

    def sub_block(it, carry):
        r = it // n_sub
        sb = it % n_sub
        row0 = pl.multiple_of(sb * BLOCK, BLOCK)
        valid = band & (in_cur | (sb > 0) | (c > 0))
        if dilation == 1:
            out_rows = pl.ds(row0, BLOCK)
        else:
            out_rows = pl.ds(row0 * dilation + r, BLOCK, stride=dilation)
        for hh in range(HEADS_PER_GROUP):
            q = q_ref[0, 0, r, pl.ds(row0, BLOCK), _head(hh)]
            k = kbuf[r, pl.ds(row0, 2 * BLOCK), _head(hh)]
            v = vbuf[r, pl.ds(row0, 2 * BLOCK), _head(hh)]
            s = lax.dot_general(q, k, (((1,), (1,)), ((), ())), preferred_element_type=F32)
            s = jnp.where(valid, s - slopes[hh] * ddist, NEG_INF)
            m = jnp.max(s, axis=-1, keepdims=True)
            p = jnp.exp(s - m)
            l = jnp.sum(p, axis=-1, keepdims=True)
            o_ref[hh, out_rows, :] = jnp.dot(p.astype(BF16), v, preferred_element_type=F32) / l
            lse_ref[hh, out_rows, :] = jnp.broadcast_to(m + jnp.log(l), (BLOCK, HEAD_DIM))
        return carry

    lax.fori_loop(0, dilation * n_sub, sub_block, 0, unroll=2)


def _attention_group(qkv, g, batch, seq):
    window, dil = DILATED_GROUPS[g]
    assert window // dil == BLOCK and seq % ATTN_CHUNK == 0 and ATTN_CHUNK % (dil * BLOCK) == 0
    rows = ATTN_CHUNK // dil
    chunks = seq // ATTN_CHUNK
    cur = lambda part: pl.BlockSpec((1, 1, dil, rows, GROUP_WIDTH), lambda b, c: (part, b, 0, c, 0))
    prev = lambda part: pl.BlockSpec((1, 1, dil, BLOCK, GROUP_WIDTH),
                                     lambda b, c: (part, b, 0, jnp.maximum(c * (rows // BLOCK) - 1, 0), 0))
    ospec = pl.BlockSpec((HEADS_PER_GROUP, ATTN_CHUNK, HEAD_DIM), lambda b, c: (0, b * chunks + c, 0))
    oshape = jax.ShapeDtypeStruct((HEADS_PER_GROUP, batch * seq, HEAD_DIM), F32)
    slopes = ALIBI_SLOPES[g * HEADS_PER_GROUP:(g + 1) * HEADS_PER_GROUP]
    return pl.pallas_call(
        functools.partial(_attn_body, slopes=slopes, dilation=dil, rows=rows),
        grid=(batch, chunks),
        in_specs=[cur(0), prev(1), cur(1), prev(2), cur(2)],
        out_specs=(ospec, ospec),
        out_shape=(oshape, oshape),
        scratch_shapes=[pltpu.VMEM((dil, BLOCK + rows, GROUP_WIDTH), BF16)] * 2,
        compiler_params=_params(("parallel", "arbitrary")),
        name=f"attention_g{g}",
    )(qkv, qkv, qkv, qkv, qkv)


def _mix_body(o0_ref, o1_ref, o2_ref, l0_ref, l1_ref, l2_ref, z_ref, zh_ref, gb_ref, sa_ref, sv_ref,
              x_ref, cw_ref, wa_ref, wc_ref, wo_ref, out_ref, *, tiles_per_seq):
    i = pl.program_id(0)
    heads = []
    for hh in range(HEADS_PER_GROUP):
        l0, l1, l2 = l0_ref[hh], l1_ref[hh], l2_ref[hh]
        m = jnp.maximum(jnp.maximum(l0, l1), l2)
        e0, e1, e2 = jnp.exp(l0 - m), jnp.exp(l1 - m), jnp.exp(l2 - m)
        heads.append(((e0 * o0_ref[hh] + e1 * o1_ref[hh] + e2 * o2_ref[hh]) / (e0 + e1 + e2)).astype(BF16))
    attn = jnp.concatenate(heads, axis=-1)
    branch_a = jnp.dot(attn, wa_ref[...], preferred_element_type=F32)

    z = z_ref[...].astype(F32)
    halo = zh_ref[...].astype(F32)
    halo = jnp.where(i % tiles_per_seq == 0, 0.0, halo)
    h1 = halo[SUBLANES - 1:SUBLANES, :]
    h2 = halo[SUBLANES - 2:SUBLANES - 1, :]
    rows = lax.broadcasted_iota(jnp.int32, z.shape, 0)
    zm1 = jnp.where(rows == 0, h1, pltpu.roll(z, 1, 0))
    zm2 = jnp.where(rows == 0, h2, jnp.where(rows == 1, h1, pltpu.roll(z, 2, 0)))
    conv = cw_ref[2:3, :] * z + cw_ref[1:2, :] * zm1 + cw_ref[0:1, :] * zm2
    y = gb_ref[...].astype(F32) * conv
    branch_b = jnp.dot(y.astype(BF16), wc_ref[...], preferred_element_type=F32)

    merged = sa_ref[...].astype(F32) * branch_a + sv_ref[...].astype(F32) * branch_b
    out_ref[...] = x_ref[...] + jnp.dot(merged.astype(BF16), wo_ref[...], preferred_element_type=F32)


def _mix(os, lses, z, gb, sa, sv, x, conv_w, wa, wc, wo, seq):
    t, d = x.shape
    assert seq % MIX_TM == 0
    row = pl.BlockSpec((MIX_TM, d), lambda i: (i, 0))
    head_rows = pl.BlockSpec((HEADS_PER_GROUP, MIX_TM, HEAD_DIM), lambda i: (0, i, 0))
    const = lambda a: pl.BlockSpec(a.shape, lambda i: (0, 0), pipeline_mode=pl.Buffered(1))
    halo = pl.BlockSpec((SUBLANES, d), lambda i: (jnp.maximum(i * (MIX_TM // SUBLANES) - 1, 0), 0))
    return pl.pallas_call(
        functools.partial(_mix_body, tiles_per_seq=seq // MIX_TM),
        grid=(t // MIX_TM,),
        in_specs=[head_rows] * 6 + [row, halo, row, row, row, row,
                                    const(conv_w), const(wa), const(wc), const(wo)],
        out_specs=row,
        out_shape=jax.ShapeDtypeStruct((t, d), F32),
        compiler_params=_params(("parallel",)),
        name="mix",
    )(*os, *lses, z, z, gb, sa, sv, x, conv_w, wa, wc, wo)


def kernel(x, ffn1_norm, ffn1_w_gate, ffn1_w_up, ffn1_w_down, mix_norm, w_in, q_norm, k_norm, conv_w,
           w_attn_out, w_conv_out, w_o, ffn2_norm, ffn2_w_gate, ffn2_w_up, ffn2_w_down):
    batch, seq, d = x.shape
    t = batch * seq
    xt = x.reshape(t, d)
    for layer in range(ffn1_norm.shape[0]):
        bf = lambda w: w[layer].astype(BF16)
        xt, h = _ffn(xt, ffn1_norm[layer], bf(ffn1_w_gate), bf(ffn1_w_up), bf(ffn1_w_down),
                     next_gain=mix_norm[layer])
        w_in_b = bf(w_in)
        qk_gain = jnp.concatenate([q_norm[layer].reshape(-1) * (HEAD_DIM ** -0.5),
                                   k_norm[layer].reshape(-1),
                                   jnp.ones((ATTN_WIDTH,), F32)]).reshape(1, QKV_WIDTH)
        z, gb, sa, sv = _gate_proj(h, w_in_b)
        os, lses = zip(*[_attention_group(_qkv_proj(h, w_in_b, qk_gain, g, batch, seq), g, batch, seq)
                         for g in range(N_GROUPS)])
        xt = _mix(os, lses, z, gb, sa, sv, xt, conv_w[layer], bf(w_attn_out), bf(w_conv_out), bf(w_o), seq)
        xt = _ffn(xt, ffn2_norm[layer], bf(ffn2_w_gate), bf(ffn2_w_up), bf(ffn2_w_down))
    return xt.reshape(batch, seq, d)
```

```python
import functools

import jax
import jax.numpy as jnp
from jax import lax
from jax.experimental import pallas as pl
from jax.experimental.pallas import tpu as pltpu

F32 = jnp.float32
BF16 = jnp.bfloat16

D_MODEL = 2048
D_FF = 5632
HEAD_DIM = 128
DILATED_GROUPS = ((128, 1), (512, 4), (2048, 16))
N_GROUPS = len(DILATED_GROUPS)
HEADS_PER_GROUP = 4
N_ATTN_HEADS = N_GROUPS * HEADS_PER_GROUP
ATTN_WIDTH = N_ATTN_HEADS * HEAD_DIM
GROUP_WIDTH = HEADS_PER_GROUP * HEAD_DIM
QKV_WIDTH = 3 * ATTN_WIDTH
CONV_WIDTH = 3
BLOCK = 128
RMS_EPS = 1e-6
NEG_INF = -1e30
ALIBI_SLOPES = tuple(2.0 ** (-8.0 * (i + 1) / N_ATTN_HEADS) for i in range(N_ATTN_HEADS))
U_OFF = QKV_WIDTH
GB_OFF = U_OFF + D_MODEL
GC_OFF = GB_OFF + D_MODEL
GA_OFF = GC_OFF + D_MODEL
GV_OFF = GA_OFF + D_MODEL

VMEM_LIMIT_BYTES = 60 * 1024 * 1024
SUBLANES = 8

FFN_TM, FFN_TF = 1024, 256
QKV_TM, QKV_SUB = 1024, 256
GATE_TM, GATE_TN = 1024, 256
ATTN_CHUNK = 2048
MIX_TM = 256


def _params(sem):
    return pltpu.CompilerParams(dimension_semantics=sem, vmem_limit_bytes=VMEM_LIMIT_BYTES)


def _rms(x, gain):
    ms = jnp.mean(x * x, axis=-1, keepdims=True)
    return x * lax.rsqrt(ms + RMS_EPS) * gain


def _head(hh):
    return slice(hh * HEAD_DIM, (hh + 1) * HEAD_DIM)


def _ffn_body(x_ref, gain_ref, wg_ref, wu_ref, wd_ref, *rest, n_ff, emit_next):
    if emit_next:
        next_gain_ref, o_ref, h_ref, xn_ref = rest
    else:
        o_ref, xn_ref = rest
    j = pl.program_id(1)

    @pl.when(j == 0)
    def _():
        xn_ref[...] = _rms(x_ref[...], gain_ref[...]).astype(BF16)
        o_ref[...] = jnp.zeros_like(o_ref)

    xn = xn_ref[...]
    g = jnp.dot(xn, wg_ref[...], preferred_element_type=F32)
    u = jnp.dot(xn, wu_ref[...], preferred_element_type=F32)
    a = (g * jax.nn.sigmoid(g) * u).astype(BF16)
    o_ref[...] += jnp.dot(a, wd_ref[...], preferred_element_type=F32)

    @pl.when(j == n_ff - 1)
    def _():
        y = x_ref[...] + 0.5 * o_ref[...]
        o_ref[...] = y
        if emit_next:
            h_ref[...] = _rms(y, next_gain_ref[...]).astype(BF16)


def _ffn(x, gain, wg, wu, wd, next_gain=None):
    t, d = x.shape
    n_ff = D_FF // FFN_TF
    emit_next = next_gain is not None
    row = pl.BlockSpec((FFN_TM, d), lambda i, j: (i, 0))
    vec = pl.BlockSpec((1, d), lambda i, j: (0, 0))
    in_specs = [row, vec,
                pl.BlockSpec((d, FFN_TF), lambda i, j: (0, j)),
                pl.BlockSpec((d, FFN_TF), lambda i, j: (0, j)),
                pl.BlockSpec((FFN_TF, d), lambda i, j: (j, 0))]
    args = [x, gain.reshape(1, d), wg, wu, wd]
    out_shape = jax.ShapeDtypeStruct((t, d), F32)
    out_specs = row
    if emit_next:
        in_specs.append(vec)
        args.append(next_gain.reshape(1, d))
        out_shape = (out_shape, jax.ShapeDtypeStruct((t, d), BF16))
        out_specs = (row, row)
    return pl.pallas_call(
        functools.partial(_ffn_body, n_ff=n_ff, emit_next=emit_next),
        grid=(t // FFN_TM, n_ff),
        in_specs=in_specs,
        out_specs=out_specs,
        out_shape=out_shape,
        scratch_shapes=[pltpu.VMEM((FFN_TM, d), BF16)],
        compiler_params=_params(("parallel", "arbitrary")),
        name="ffn",
    )(*args)


def _qkv_body(h_ref, w_ref, qg_ref, kg_ref, o_ref, scr_ref, *, dilation):
    w = w_ref[...]
    gains = (qg_ref, kg_ref)
    sub = QKV_SUB // dilation
    for ch in range(QKV_TM // QKV_SUB):
        tok = pl.ds(ch * QKV_SUB, QKV_SUB)
        acc = jnp.dot(h_ref[tok, :], w, preferred_element_type=F32)
        for part in range(3):
            for hh in range(HEADS_PER_GROUP):
                val = acc[:, part * GROUP_WIDTH + hh * HEAD_DIM:part * GROUP_WIDTH + (hh + 1) * HEAD_DIM]
                if part < 2:
                    val = _rms(val, gains[part][:, _head(hh)])
                if dilation == 1:
                    o_ref[part, 0, 0, tok, _head(hh)] = val.astype(BF16)
                else:
                    scr_ref[part * HEADS_PER_GROUP + hh, tok, :] = val
        if dilation > 1:
            for part in range(3):
                for r in range(dilation):
                    for hh in range(HEADS_PER_GROUP):
                        src = scr_ref[part * HEADS_PER_GROUP + hh, pl.ds(ch * QKV_SUB + r, sub, stride=dilation), :]
                        o_ref[part, 0, r, pl.ds(ch * sub, sub), _head(hh)] = src.astype(BF16)


def _qkv_proj(h, w_g, qk_gain, g, batch, seq):
    _, dil = DILATED_GROUPS[g]
    t, d = h.shape
    tiles_per_seq = seq // QKV_TM
    rows = QKV_TM // dil
    return pl.pallas_call(
        functools.partial(_qkv_body, dilation=dil),
        grid=(t // QKV_TM,),
        in_specs=[pl.BlockSpec((QKV_TM, d), lambda i: (i, 0)),
                  pl.BlockSpec((d, 3 * GROUP_WIDTH), lambda i: (0, 0), pipeline_mode=pl.Buffered(1)),
                  pl.BlockSpec((1, GROUP_WIDTH), lambda i: (0, g)),
                  pl.BlockSpec((1, GROUP_WIDTH), lambda i: (0, N_GROUPS + g))],
        out_specs=pl.BlockSpec((3, 1, dil, rows, GROUP_WIDTH),
                               lambda i: (0, i // tiles_per_seq, 0, i % tiles_per_seq, 0)),
        out_shape=jax.ShapeDtypeStruct((3, batch, dil, seq // dil, GROUP_WIDTH), BF16),
        scratch_shapes=[pltpu.VMEM((3 * HEADS_PER_GROUP, QKV_TM, HEAD_DIM), F32)],
        compiler_params=_params(("parallel",)),
        name=f"qkv_proj_g{g}",
    )(h, w_g, qk_gain, qk_gain)


def _gate_body(h_ref, wu_ref, wb_ref, wc_ref, wa_ref, wv_ref, z_ref, gb_ref, sa_ref, sv_ref):
    h = h_ref[...]
    dot = lambda w_ref: jnp.dot(h, w_ref[...], preferred_element_type=F32)
    z_ref[...] = (dot(wc_ref) * dot(wu_ref)).astype(BF16)
    gb_ref[...] = dot(wb_ref).astype(BF16)
    sa_ref[...] = jax.nn.sigmoid(dot(wa_ref)).astype(BF16)
    sv_ref[...] = jax.nn.sigmoid(dot(wv_ref)).astype(BF16)


def _gate_proj(h, w_in):
    t, d = h.shape
    wspec = lambda off: pl.BlockSpec((d, GATE_TN), lambda i, j: (0, off // GATE_TN + j))
    ospec = pl.BlockSpec((GATE_TM, GATE_TN), lambda i, j: (i, j))
    oshape = jax.ShapeDtypeStruct((t, d), BF16)
    return pl.pallas_call(
        _gate_body,
        grid=(t // GATE_TM, d // GATE_TN),
        in_specs=[pl.BlockSpec((GATE_TM, d), lambda i, j: (i, 0)),
                  wspec(U_OFF), wspec(GB_OFF), wspec(GC_OFF), wspec(GA_OFF), wspec(GV_OFF)],
        out_specs=(ospec,) * 4,
        out_shape=(oshape,) * 4,
        compiler_params=_params(("parallel", "arbitrary")),
        name="gate_proj",
    )(h, w_in, w_in, w_in, w_in, w_in)


def _attn_body(q_ref, kp_ref, kc_ref, vp_ref, vc_ref, o_ref, lse_ref, kbuf, vbuf, *, slopes, dilation, rows):
    c = pl.program_id(1)
    n_sub = rows // BLOCK
    kbuf[:, :BLOCK, :] = kp_ref[0, 0]
    kbuf[:, BLOCK:, :] = kc_ref[0, 0]
    vbuf[:, :BLOCK, :] = vp_ref[0, 0]
    vbuf[:, BLOCK:, :] = vc_ref[0, 0]

    qi = lax.broadcasted_iota(jnp.int32, (BLOCK, 2 * BLOCK), 0)
    ki = lax.broadcasted_iota(jnp.int32, (BLOCK, 2 * BLOCK), 1)
    dist = BLOCK + qi - ki
    band = (dist >= 0) & (dist <= BLOCK)
    in_cur = ki >= BLOCK
    ddist = (dilation * dist).astype(F32)

    def sub_block(it, carry):
        r = it // n_sub
        sb = it % n_sub
        row0 = pl.multiple_of(sb * BLOCK, BLOCK)
        valid = band & (in_cur | (sb > 0) | (c > 0))
        if dilation == 1:
            out_rows = pl.ds(row0, BLOCK)
        else:
            out_rows = pl.ds(row0 * dilation + r, BLOCK, stride=dilation)
        for hh in range(HEADS_PER_GROUP):
            q = q_ref[0, 0, r, pl.ds(row0, BLOCK), _head(hh)]
            k = kbuf[r, pl.ds(row0, 2 * BLOCK), _head(hh)]
            v = vbuf[r, pl.ds(row0, 2 * BLOCK), _head(hh)]
            s = lax.dot_general(q, k, (((1,), (1,)), ((), ())), preferred_element_type=F32)
            s = jnp.where(valid, s - slopes[hh] * ddist, NEG_INF)
            m = jnp.max(s, axis=-1, keepdims=True)
            p = jnp.exp(s - m)
            l = jnp.sum(p, axis=-1, keepdims=True)
            o_ref[hh, out_rows, :] = jnp.dot(p.astype(BF16), v, preferred_element_type=F32) / l
            lse_ref[hh, out_rows, :] = jnp.broadcast_to(m + jnp.log(l), (BLOCK, HEAD_DIM))
        return carry

    lax.fori_loop(0, dilation * n_sub, sub_block, 0, unroll=2)


def _attention_group(qkv, g, batch, seq):
    window, dil = DILATED_GROUPS[g]
    assert window // dil == BLOCK and seq % ATTN_CHUNK == 0 and ATTN_CHUNK % (dil * BLOCK) == 0
    rows = ATTN_CHUNK // dil
    chunks = seq // ATTN_CHUNK
    cur = lambda part: pl.BlockSpec((1, 1, dil, rows, GROUP_WIDTH), lambda b, c: (part, b, 0, c, 0))
    prev = lambda part: pl.BlockSpec((1, 1, dil, BLOCK, GROUP_WIDTH),
                                     lambda b, c: (part, b, 0, jnp.maximum(c * (rows // BLOCK) - 1, 0), 0))
    ospec = pl.BlockSpec((HEADS_PER_GROUP, ATTN_CHUNK, HEAD_DIM), lambda b, c: (0, b * chunks + c, 0))
    oshape = jax.ShapeDtypeStruct((HEADS_PER_GROUP, batch * seq, HEAD_DIM), F32)
    slopes = ALIBI_SLOPES[g * HEADS_PER_GROUP:(g + 1) * HEADS_PER_GROUP]
    return pl.pallas_call(
        functools.partial(_attn_body, slopes=slopes, dilation=dil, rows=rows),
        grid=(batch, chunks),
        in_specs=[cur(0), prev(1), cur(1), prev(2), cur(2)],
        out_specs=(ospec, ospec),
        out_shape=(oshape, oshape),
        scratch_shapes=[pltpu.VMEM((dil, BLOCK + rows, GROUP_WIDTH), BF16)] * 2,
        compiler_params=_params(("parallel", "arbitrary")),
        name=f"attention_g{g}",
    )(qkv, qkv, qkv, qkv, qkv)


def _mix_body(o0_ref, o1_ref, o2_ref, l0_ref, l1_ref, l2_ref, z_ref, zh_ref, gb_ref, sa_ref, sv_ref,
              x_ref, cw_ref, wa_ref, wc_ref, wo_ref, out_ref, *, tiles_per_seq):
    i = pl.program_id(0)
    heads = []
    for hh in range(HEADS_PER_GROUP):
        l0, l1, l2 = l0_ref[hh], l1_ref[hh], l2_ref[hh]
        m = jnp.maximum(jnp.maximum(l0, l1), l2)
        e0, e1, e2 = jnp.exp(l0 - m), jnp.exp(l1 - m), jnp.exp(l2 - m)
        heads.append(((e0 * o0_ref[hh] + e1 * o1_ref[hh] + e2 * o2_ref[hh]) / (e0 + e1 + e2)).astype(BF16))
    attn = jnp.concatenate(heads, axis=-1)
    branch_a = jnp.dot(attn, wa_ref[...], preferred_element_type=F32)

    z = z_ref[...].astype(F32)
    halo = zh_ref[...].astype(F32)
    halo = jnp.where(i % tiles_per_seq == 0, 0.0, halo)
    h1 = halo[SUBLANES - 1:SUBLANES, :]
    h2 = halo[SUBLANES - 2:SUBLANES - 1, :]
    rows = lax.broadcasted_iota(jnp.int32, z.shape, 0)
    zm1 = jnp.where(rows == 0, h1, pltpu.roll(z, 1, 0))
    zm2 = jnp.where(rows == 0, h2, jnp.where(rows == 1, h1, pltpu.roll(z, 2, 0)))
    conv = cw_ref[2:3, :] * z + cw_ref[1:2, :] * zm1 + cw_ref[0:1, :] * zm2
    y = gb_ref[...].astype(F32) * conv
    branch_b = jnp.dot(y.astype(BF16), wc_ref[...], preferred_element_type=F32)

    merged = sa_ref[...].astype(F32) * branch_a + sv_ref[...].astype(F32) * branch_b
    out_ref[...] = x_ref[...] + jnp.dot(merged.astype(BF16), wo_ref[...], preferred_element_type=F32)


def _mix(os, lses, z, gb, sa, sv, x, conv_w, wa, wc, wo, seq):
    t, d = x.shape
    assert seq % MIX_TM == 0
    row = pl.BlockSpec((MIX_TM, d), lambda i: (i, 0))
    head_rows = pl.BlockSpec((HEADS_PER_GROUP, MIX_TM, HEAD_DIM), lambda i: (0, i, 0))
    const = lambda a: pl.BlockSpec(a.shape, lambda i: (0, 0), pipeline_mode=pl.Buffered(1))
    halo = pl.BlockSpec((SUBLANES, d), lambda i: (jnp.maximum(i * (MIX_TM // SUBLANES) - 1, 0), 0))
    return pl.pallas_call(
        functools.partial(_mix_body, tiles_per_seq=seq // MIX_TM),
        grid=(t // MIX_TM,),
        in_specs=[head_rows] * 6 + [row, halo, row, row, row, row,
                                    const(conv_w), const(wa), const(wc), const(wo)],
        out_specs=row,
        out_shape=jax.ShapeDtypeStruct((t, d), F32),
        compiler_params=_params(("parallel",)),
        name="mix",
    )(*os, *lses, z, z, gb, sa, sv, x, conv_w, wa, wc, wo)


def kernel(x, ffn1_norm, ffn1_w_gate, ffn1_w_up, ffn1_w_down, mix_norm, w_in, q_norm, k_norm, conv_w,
           w_attn_out, w_conv_out, w_o, ffn2_norm, ffn2_w_gate, ffn2_w_up, ffn2_w_down):
    batch, seq, d = x.shape
    t = batch * seq
    xt = x.reshape(t, d)
    for layer in range(ffn1_norm.shape[0]):
        bf = lambda w: w[layer].astype(BF16)
        xt, h = _ffn(xt, ffn1_norm[layer], bf(ffn1_w_gate), bf(ffn1_w_up), bf(ffn1_w_down),
                     next_gain=mix_norm[layer])
        w_in_b = bf(w_in)
        qk_gain = jnp.concatenate([q_norm[layer].reshape(-1) * (HEAD_DIM ** -0.5),
                                   k_norm[layer].reshape(-1)]).reshape(1, 2 * ATTN_WIDTH)
        z, gb, sa, sv = _gate_proj(h, w_in_b)
        os, lses = [], []
        for g in range(N_GROUPS):
            w_g = jnp.concatenate([w_in_b[:, p * ATTN_WIDTH + g * GROUP_WIDTH:p * ATTN_WIDTH + (g + 1) * GROUP_WIDTH]
                                   for p in range(3)], axis=1)
            o_g, lse_g = _attention_group(_qkv_proj(h, w_g, qk_gain, g, batch, seq), g, batch, seq)
            os.append(o_g)
            lses.append(lse_g)
        xt = _mix(os, lses, z, gb, sa, sv, xt, conv_w[layer], bf(w_attn_out), bf(w_conv_out), bf(w_o), seq)
        xt = _ffn(xt, ffn2_norm[layer], bf(ffn2_w_gate), bf(ffn2_w_up), bf(ffn2_w_down))
    return xt.reshape(batch, seq, d)
```

```python
import functools

import jax
import jax.numpy as jnp
from jax import lax
from jax.experimental import pallas as pl
from jax.experimental.pallas import tpu as pltpu

F32 = jnp.float32
BF16 = jnp.bfloat16

D_MODEL = 2048
D_FF = 5632
HEAD_DIM = 128
DILATED_GROUPS = ((128, 1), (512, 4), (2048, 16))
N_GROUPS = len(DILATED_GROUPS)
HEADS_PER_GROUP = 4
N_ATTN_HEADS = N_GROUPS * HEADS_PER_GROUP
ATTN_WIDTH = N_ATTN_HEADS * HEAD_DIM
GROUP_WIDTH = HEADS_PER_GROUP * HEAD_DIM
QKV_WIDTH = 3 * ATTN_WIDTH
CONV_WIDTH = 3
BLOCK = 128
RMS_EPS = 1e-6
NEG_INF = -1e30
ALIBI_SLOPES = tuple(2.0 ** (-8.0 * (i + 1) / N_ATTN_HEADS) for i in range(N_ATTN_HEADS))
U_OFF = QKV_WIDTH
GB_OFF = U_OFF + D_MODEL
GC_OFF = GB_OFF + D_MODEL
GA_OFF = GC_OFF + D_MODEL
GV_OFF = GA_OFF + D_MODEL

VMEM_LIMIT_BYTES = 60 * 1024 * 1024
SUBLANES = 8

FFN_TM, FFN_TF = 1024, 512
QKV_TM, QKV_SUB = 1024, 256
GATE_TM, GATE_TN = 1024, 512
ATTN_CHUNK = 2048
MIX_TM = 256


def _params(sem):
    return pltpu.CompilerParams(dimension_semantics=sem, vmem_limit_bytes=VMEM_LIMIT_BYTES)


def _rms(x, gain):
    ms = jnp.mean(x * x, axis=-1, keepdims=True)
    return x * lax.rsqrt(ms + RMS_EPS) * gain


def _head(hh):
    return slice(hh * HEAD_DIM, (hh + 1) * HEAD_DIM)


def _ffn_body(x_ref, gain_ref, wg_ref, wu_ref, wd_ref, o_ref, xn_ref, *, n_ff):
    j = pl.program_id(1)

    @pl.when(j == 0)
    def _():
        xn_ref[...] = _rms(x_ref[...], gain_ref[...]).astype(BF16)
        o_ref[...] = jnp.zeros_like(o_ref)

    xn = xn_ref[...]
    g = jnp.dot(xn, wg_ref[...], preferred_element_type=F32)
    u = jnp.dot(xn, wu_ref[...], preferred_element_type=F32)
    a = (g * jax.nn.sigmoid(g) * u).astype(BF16)
    o_ref[...] += jnp.dot(a, wd_ref[...], preferred_element_type=F32)

    @pl.when(j == n_ff - 1)
    def _():
        o_ref[...] = x_ref[...] + 0.5 * o_ref[...]


def _ffn(x, gain, wg, wu, wd):
    t, d = x.shape
    n_ff = D_FF // FFN_TF
    row = pl.BlockSpec((FFN_TM, d), lambda i, j: (i, 0))
    return pl.pallas_call(
        functools.partial(_ffn_body, n_ff=n_ff),
        grid=(t // FFN_TM, n_ff),
        in_specs=[row,
                  pl.BlockSpec((1, d), lambda i, j: (0, 0)),
                  pl.BlockSpec((d, FFN_TF), lambda i, j: (0, j)),
                  pl.BlockSpec((d, FFN_TF), lambda i, j: (0, j)),
                  pl.BlockSpec((FFN_TF, d), lambda i, j: (j, 0))],
        out_specs=row,
        out_shape=jax.ShapeDtypeStruct((t, d), F32),
        scratch_shapes=[pltpu.VMEM((FFN_TM, d), BF16)],
        compiler_params=_params(("parallel", "arbitrary")),
        name="ffn",
    )(x, gain.reshape(1, d), wg, wu, wd)


def _qkv_body(*refs, dilation, from_x):
    if from_x:
        x_ref, ng_ref, w_ref, qg_ref, kg_ref, o_ref, h_ref, scr_ref = refs
    else:
        h_ref, w_ref, qg_ref, kg_ref, o_ref, scr_ref = refs
    w = w_ref[...]
    gains = (qg_ref, kg_ref)
    sub = QKV_SUB // dilation
    for ch in range(QKV_TM // QKV_SUB):
        tok = pl.ds(ch * QKV_SUB, QKV_SUB)
        if from_x:
            h = _rms(x_ref[tok, :], ng_ref[...]).astype(BF16)
            h_ref[tok, :] = h
        else:
            h = h_ref[tok, :]
        acc = jnp.dot(h, w, preferred_element_type=F32)
        for part in range(3):
            for hh in range(HEADS_PER_GROUP):
                val = acc[:, part * GROUP_WIDTH + hh * HEAD_DIM:part * GROUP_WIDTH + (hh + 1) * HEAD_DIM]
                if part < 2:
                    val = _rms(val, gains[part][:, _head(hh)])
                if dilation == 1:
                    o_ref[part, 0, 0, tok, _head(hh)] = val.astype(BF16)
                else:
                    scr_ref[part * HEADS_PER_GROUP + hh, tok, :] = val
        if dilation > 1:
            for part in range(3):
                for r in range(dilation):
                    for hh in range(HEADS_PER_GROUP):
                        src = scr_ref[part * HEADS_PER_GROUP + hh, pl.ds(ch * QKV_SUB + r, sub, stride=dilation), :]
                        o_ref[part, 0, r, pl.ds(ch * sub, sub), _head(hh)] = src.astype(BF16)


def _qkv_proj(h, w_g, qk_gain, g, batch, seq, pre_norm_gain=None):
    _, dil = DILATED_GROUPS[g]
    t, d = h.shape
    tiles_per_seq = seq // QKV_TM
    rows = QKV_TM // dil
    from_x = pre_norm_gain is not None
    row = pl.BlockSpec((QKV_TM, d), lambda i: (i, 0))
    in_specs = [row]
    args = [h]
    if from_x:
        in_specs.append(pl.BlockSpec((1, d), lambda i: (0, 0)))
        args.append(pre_norm_gain.reshape(1, d))
    in_specs += [pl.BlockSpec((d, 3 * GROUP_WIDTH), lambda i: (0, 0), pipeline_mode=pl.Buffered(1)),
                 pl.BlockSpec((1, GROUP_WIDTH), lambda i: (0, g)),
                 pl.BlockSpec((1, GROUP_WIDTH), lambda i: (0, N_GROUPS + g))]
    args += [w_g, qk_gain, qk_gain]
    out_specs = pl.BlockSpec((3, 1, dil, rows, GROUP_WIDTH),
                             lambda i: (0, i // tiles_per_seq, 0, i % tiles_per_seq, 0))
    out_shape = jax.ShapeDtypeStruct((3, batch, dil, seq // dil, GROUP_WIDTH), BF16)
    if from_x:
        out_specs = (out_specs, row)
        out_shape = (out_shape, jax.ShapeDtypeStruct((t, d), BF16))
    return pl.pallas_call(
        functools.partial(_qkv_body, dilation=dil, from_x=from_x),
        grid=(t // QKV_TM,),
        in_specs=in_specs,
        out_specs=out_specs,
        out_shape=out_shape,
        scratch_shapes=[pltpu.VMEM((3 * HEADS_PER_GROUP, QKV_TM, HEAD_DIM), F32)],
        compiler_params=_params(("parallel",)),
        name=f"qkv_proj_g{g}",
    )(*args)


def _gate_body(h_ref, wu_ref, wb_ref, wc_ref, wa_ref, wv_ref, z_ref, gb_ref, sa_ref, sv_ref):
    h = h_ref[...]
    dot = lambda w_ref: jnp.dot(h, w_ref[...], preferred_element_type=F32)
    z_ref[...] = (dot(wc_ref) * dot(wu_ref)).astype(BF16)
    gb_ref[...] = dot(wb_ref).astype(BF16)
    sa_ref[...] = jax.nn.sigmoid(dot(wa_ref)).astype(BF16)
    sv_ref[...] = jax.nn.sigmoid(dot(wv_ref)).astype(BF16)


def _gate_proj(h, w_in):
    t, d = h.shape
    wspec = lambda off: pl.BlockSpec((d, GATE_TN), lambda i, j: (0, off // GATE_TN + j))
    ospec = pl.BlockSpec((GATE_TM, GATE_TN), lambda i, j: (i, j))
    oshape = jax.ShapeDtypeStruct((t, d), BF16)
    return pl.pallas_call(
        _gate_body,
        grid=(t // GATE_TM, d // GATE_TN),
        in_specs=[pl.BlockSpec((GATE_TM, d), lambda i, j: (i, 0)),
                  wspec(U_OFF), wspec(GB_OFF), wspec(GC_OFF), wspec(GA_OFF), wspec(GV_OFF)],
        out_specs=(ospec,) * 4,
        out_shape=(oshape,) * 4,
        compiler_params=_params(("parallel", "arbitrary")),
        name="gate_proj",
    )(h, w_in, w_in, w_in, w_in, w_in)


def _attn_body(q_ref, kp_ref, kc_ref, vp_ref, vc_ref, o_ref, lse_ref, kbuf, vbuf, *, slopes, dilation, rows):
    c = pl.program_id(1)
    n_sub = rows // BLOCK
    kbuf[:, :BLOCK, :] = kp_ref[0, 0]
    kbuf[:, BLOCK:, :] = kc_ref[0, 0]
    vbuf[:, :BLOCK, :] = vp_ref[0, 0]
    vbuf[:, BLOCK:, :] = vc_ref[0, 0]

    qi = lax.broadcasted_iota(jnp.int32, (BLOCK, 2 * BLOCK), 0)
    ki = lax.broadcasted_iota(jnp.int32, (BLOCK, 2 * BLOCK), 1)
    dist = BLOCK + qi - ki
    band = (dist >= 0) & (dist <= BLOCK)
    in_cur = ki >= BLOCK
    ddist = (dilation * dist).astype(F32)

    def sub_block(it, carry):
        r = it // n_sub
        sb = it % n_sub
        row0 = pl.multiple_of(sb * BLOCK, BLOCK)
        valid = band & (in_cur | (sb > 0) | (c > 0))
        if dilation == 1:
            out_rows = pl.ds(row0, BLOCK)
        else:
            out_rows = pl.ds(row0 * dilation + r, BLOCK, stride=dilation)
        for hh in range(HEADS_PER_GROUP):
            q = q_ref[0, 0, r, pl.ds(row0, BLOCK), _head(hh)]
            k = kbuf[r, pl.ds(row0, 2 * BLOCK), _head(hh)]
            v = vbuf[r, pl.ds(row0, 2 * BLOCK), _head(hh)]
            s = lax.dot_general(q, k, (((1,), (1,)), ((), ())), preferred_element_type=F32)
            s = jnp.where(valid, s - slopes[hh] * ddist, NEG_INF)
            m = jnp.max(s, axis=-1, keepdims=True)
            p = jnp.exp(s - m)
            l = jnp.sum(p, axis=-1, keepdims=True)
            o_ref[hh, out_rows, :] = jnp.dot(p.astype(BF16), v, preferred_element_type=F32) / l
            lse_ref[hh, out_rows, :] = jnp.broadcast_to(m + jnp.log(l), (BLOCK, HEAD_DIM))
        return carry

    lax.fori_loop(0, dilation * n_sub, sub_block, 0, unroll=2)


def _attention_group(qkv, g, batch, seq):
    window, dil = DILATED_GROUPS[g]
    assert window // dil == BLOCK and seq % ATTN_CHUNK == 0 and ATTN_CHUNK % (dil * BLOCK) == 0
    rows = ATTN_CHUNK // dil
    chunks = seq // ATTN_CHUNK
    cur = lambda part: pl.BlockSpec((1, 1, dil, rows, GROUP_WIDTH), lambda b, c: (part, b, 0, c, 0))
    prev = lambda part: pl.BlockSpec((1, 1, dil, BLOCK, GROUP_WIDTH),
                                     lambda b, c: (part, b, 0, jnp.maximum(c * (rows // BLOCK) - 1, 0), 0))
    ospec = pl.BlockSpec((HEADS_PER_GROUP, ATTN_CHUNK, HEAD_DIM), lambda b, c: (0, b * chunks + c, 0))
    oshape = jax.ShapeDtypeStruct((HEADS_PER_GROUP, batch * seq, HEAD_DIM), F32)
    slopes = ALIBI_SLOPES[g * HEADS_PER_GROUP:(g + 1) * HEADS_PER_GROUP]
    return pl.pallas_call(
        functools.partial(_attn_body, slopes=slopes, dilation=dil, rows=rows),
        grid=(batch, chunks),
        in_specs=[cur(0), prev(1), cur(1), prev(2), cur(2)],
        out_specs=(ospec, ospec),
        out_shape=(oshape, oshape),
        scratch_shapes=[pltpu.VMEM((dil, BLOCK + rows, GROUP_WIDTH), BF16)] * 2,
        compiler_params=_params(("parallel", "arbitrary")),
        name=f"attention_g{g}",
    )(qkv, qkv, qkv, qkv, qkv)


def _mix_body(o0_ref, o1_ref, o2_ref, l0_ref, l1_ref, l2_ref, z_ref, zh_ref, gb_ref, sa_ref, sv_ref,
              x_ref, cw_ref, wa_ref, wc_ref, wo_ref, out_ref, *, tiles_per_seq):
    i = pl.program_id(0)
    z = z_ref[...].astype(F32)
    halo = zh_ref[...].astype(F32)
    halo = jnp.where(i % tiles_per_seq == 0, 0.0, halo)
    h1 = halo[SUBLANES - 1:SUBLANES, :]
    h2 = halo[SUBLANES - 2:SUBLANES - 1, :]
    rows = lax.broadcasted_iota(jnp.int32, z.shape, 0)
    zm1 = jnp.where(rows == 0, h1, pltpu.roll(z, 1, 0))
    zm2 = jnp.where(rows == 0, h2, jnp.where(rows == 1, h1, pltpu.roll(z, 2, 0)))
    conv = cw_ref[2:3, :] * z + cw_ref[1:2, :] * zm1 + cw_ref[0:1, :] * zm2
    y = (gb_ref[...].astype(F32) * conv).astype(BF16)
    branch_b = jnp.dot(y, wc_ref[...], preferred_element_type=F32)

    heads = []
    for hh in range(HEADS_PER_GROUP):
        l0, l1, l2 = l0_ref[hh], l1_ref[hh], l2_ref[hh]
        m = jnp.maximum(jnp.maximum(l0, l1), l2)
        e0, e1, e2 = jnp.exp(l0 - m), jnp.exp(l1 - m), jnp.exp(l2 - m)
        heads.append(((e0 * o0_ref[hh] + e1 * o1_ref[hh] + e2 * o2_ref[hh]) / (e0 + e1 + e2)).astype(BF16))
    attn = jnp.concatenate(heads, axis=-1)
    branch_a = jnp.dot(attn, wa_ref[...], preferred_element_type=F32)

    merged = sa_ref[...].astype(F32) * branch_a + sv_ref[...].astype(F32) * branch_b
    out_ref[...] = x_ref[...] + jnp.dot(merged.astype(BF16), wo_ref[...], preferred_element_type=F32)


def _mix(os, lses, z, gb, sa, sv, x, conv_w, wa, wc, wo, seq):
    t, d = x.shape
    assert seq % MIX_TM == 0
    row = pl.BlockSpec((MIX_TM, d), lambda i: (i, 0))
    head_rows = pl.BlockSpec((HEADS_PER_GROUP, MIX_TM, HEAD_DIM), lambda i: (0, i, 0))
    const = lambda a: pl.BlockSpec(a.shape, lambda i: (0, 0), pipeline_mode=pl.Buffered(1))
    halo = pl.BlockSpec((SUBLANES, d), lambda i: (jnp.maximum(i * (MIX_TM // SUBLANES) - 1, 0), 0))
    return pl.pallas_call(
        functools.partial(_mix_body, tiles_per_seq=seq // MIX_TM),
        grid=(t // MIX_TM,),
        in_specs=[head_rows] * 6 + [row, halo, row, row, row, row,
                                    const(conv_w), const(wa), const(wc), const(wo)],
        out_specs=row,
        out_shape=jax.ShapeDtypeStruct((t, d), F32),
        compiler_params=_params(("parallel",)),
        name="mix",
    )(*os, *lses, z, z, gb, sa, sv, x, conv_w, wa, wc, wo)


def kernel(x, ffn1_norm, ffn1_w_gate, ffn1_w_up, ffn1_w_down, mix_norm, w_in, q_norm, k_norm, conv_w,
           w_attn_out, w_conv_out, w_o, ffn2_norm, ffn2_w_gate, ffn2_w_up, ffn2_w_down):
    batch, seq, d = x.shape
    t = batch * seq
    xt = x.reshape(t, d)
    for layer in range(ffn1_norm.shape[0]):
        bf = lambda w: w[layer].astype(BF16)
        xt = _ffn(xt, ffn1_norm[layer], bf(ffn1_w_gate), bf(ffn1_w_up), bf(ffn1_w_down))
        w_in_b = bf(w_in)
        qk_gain = jnp.concatenate([q_norm[layer].reshape(-1) * (HEAD_DIM ** -0.5),
                                   k_norm[layer].reshape(-1)]).reshape(1, 2 * ATTN_WIDTH)
        os, lses = [], []
        h = None
        for g in range(N_GROUPS):
            w_g = jnp.concatenate([w_in_b[:, p * ATTN_WIDTH + g * GROUP_WIDTH:p * ATTN_WIDTH + (g + 1) * GROUP_WIDTH]
                                   for p in range(3)], axis=1)
            if h is None:
                qkv, h = _qkv_proj(xt, w_g, qk_gain, g, batch, seq, pre_norm_gain=mix_norm[layer])
            else:
                qkv = _qkv_proj(h, w_g, qk_gain, g, batch, seq)
            o_g, lse_g = _attention_group(qkv, g, batch, seq)
            os.append(o_g)
            lses.append(lse_g)
        z, gb, sa, sv = _gate_proj(h, w_in_b)
        xt = _mix(os, lses, z, gb, sa, sv, xt, conv_w[layer], bf(w_attn_out), bf(w_conv_out), bf(w_o), seq)
        xt = _ffn(xt, ffn2_norm[layer], bf(ffn2_w_gate), bf(ffn2_w_up), bf(ffn2_w_down))
    return xt.reshape(batch, seq, d)
```

```python
import functools

import jax
import jax.numpy as jnp
from jax import lax
from jax.experimental import pallas as pl
from jax.experimental.pallas import tpu as pltpu

F32 = jnp.float32
BF16 = jnp.bfloat16

D_MODEL = 2048
D_FF = 5632
HEAD_DIM = 128
DILATED_GROUPS = ((128, 1), (512, 4), (2048, 16))
N_GROUPS = len(DILATED_GROUPS)
HEADS_PER_GROUP = 4
N_ATTN_HEADS = N_GROUPS * HEADS_PER_GROUP
ATTN_WIDTH = N_ATTN_HEADS * HEAD_DIM
GROUP_WIDTH = HEADS_PER_GROUP * HEAD_DIM
QKV_WIDTH = 3 * ATTN_WIDTH
CONV_WIDTH = 3
BLOCK = 128
RMS_EPS = 1e-6
NEG_INF = -1e30
ALIBI_SLOPES = tuple(2.0 ** (-8.0 * (i + 1) / N_ATTN_HEADS) for i in range(N_ATTN_HEADS))
U_OFF = QKV_WIDTH
GB_OFF = U_OFF + D_MODEL
GC_OFF = GB_OFF + D_MODEL
GA_OFF = GC_OFF + D_MODEL
GV_OFF = GA_OFF + D_MODEL

VMEM_LIMIT_BYTES = 60 * 1024 * 1024
SUBLANES = 8

FFN_TM, FFN_TF = 1024, 512
QKV_TM, QKV_SUB = 1024, 256
GATE_TM, GATE_TN = 1024, 512
ATTN_CHUNK = 2048
MIX_TM = 256


def _params(sem):
    return pltpu.CompilerParams(dimension_semantics=sem, vmem_limit_bytes=VMEM_LIMIT_BYTES)


def _rms(x, gain):
    ms = jnp.mean(x * x, axis=-1, keepdims=True)
    return x * lax.rsqrt(ms + RMS_EPS) * gain


def _head(hh):
    return slice(hh * HEAD_DIM, (hh + 1) * HEAD_DIM)


def _ffn_body(x_ref, gain_ref, wg_ref, wu_ref, wd_ref, o_ref, xn_ref):
    @pl.when(pl.program_id(1) == 0)
    def _():
        x = x_ref[...]
        xn_ref[...] = _rms(x, gain_ref[...]).astype(BF16)
        o_ref[...] = x

    xn = xn_ref[...]
    g = jnp.dot(xn, wg_ref[...], preferred_element_type=F32)
    u = jnp.dot(xn, wu_ref[...], preferred_element_type=F32)
    a = (g * jax.nn.sigmoid(g) * (0.5 * u)).astype(BF16)
    o_ref[...] += jnp.dot(a, wd_ref[...], preferred_element_type=F32)


def _ffn(x, gain, wg, wu, wd):
    t, d = x.shape
    row = pl.BlockSpec((FFN_TM, d), lambda i, j: (i, 0))
    return pl.pallas_call(
        _ffn_body,
        grid=(t // FFN_TM, D_FF // FFN_TF),
        in_specs=[row,
                  pl.BlockSpec((1, d), lambda i, j: (0, 0)),
                  pl.BlockSpec((d, FFN_TF), lambda i, j: (0, j)),
                  pl.BlockSpec((d, FFN_TF), lambda i, j: (0, j)),
                  pl.BlockSpec((FFN_TF, d), lambda i, j: (j, 0))],
        out_specs=row,
        out_shape=jax.ShapeDtypeStruct((t, d), F32),
        scratch_shapes=[pltpu.VMEM((FFN_TM, d), BF16)],
        compiler_params=_params(("parallel", "arbitrary")),
        name="ffn",
    )(x, gain.reshape(1, d), wg, wu, wd)


def _qkv_body(*refs, dilation, from_x):
    if from_x:
        x_ref, ng_ref, w_ref, qg_ref, kg_ref, o_ref, h_ref, scr_ref = refs
    else:
        h_ref, w_ref, qg_ref, kg_ref, o_ref, scr_ref = refs
    w = w_ref[...]
    gains = (qg_ref, kg_ref)
    sub = QKV_SUB // dilation
    for ch in range(QKV_TM // QKV_SUB):
        tok = pl.ds(ch * QKV_SUB, QKV_SUB)
        if from_x:
            h = _rms(x_ref[tok, :], ng_ref[...]).astype(BF16)
            h_ref[tok, :] = h
        else:
            h = h_ref[tok, :]
        acc = jnp.dot(h, w, preferred_element_type=F32)
        for part in range(3):
            for hh in range(HEADS_PER_GROUP):
                val = acc[:, part * GROUP_WIDTH + hh * HEAD_DIM:part * GROUP_WIDTH + (hh + 1) * HEAD_DIM]
                if part < 2:
                    val = _rms(val, gains[part][:, _head(hh)])
                if dilation == 1:
                    o_ref[part, 0, 0, tok, _head(hh)] = val.astype(BF16)
                else:
                    scr_ref[part * HEADS_PER_GROUP + hh, tok, :] = val
        if dilation > 1:
            for part in range(3):
                for r in range(dilation):
                    for hh in range(HEADS_PER_GROUP):
                        src = scr_ref[part * HEADS_PER_GROUP + hh, pl.ds(ch * QKV_SUB + r, sub, stride=dilation), :]
                        o_ref[part, 0, r, pl.ds(ch * sub, sub), _head(hh)] = src.astype(BF16)


def _qkv_proj(h, w_g, qk_gain, g, batch, seq, pre_norm_gain=None):
    _, dil = DILATED_GROUPS[g]
    t, d = h.shape
    tiles_per_seq = seq // QKV_TM
    rows = QKV_TM // dil
    from_x = pre_norm_gain is not None
    row = pl.BlockSpec((QKV_TM, d), lambda i: (i, 0))
    in_specs = [row]
    args = [h]
    if from_x:
        in_specs.append(pl.BlockSpec((1, d), lambda i: (0, 0)))
        args.append(pre_norm_gain.reshape(1, d))
    in_specs += [pl.BlockSpec((d, 3 * GROUP_WIDTH), lambda i: (0, 0), pipeline_mode=pl.Buffered(1)),
                 pl.BlockSpec((1, GROUP_WIDTH), lambda i: (0, g)),
                 pl.BlockSpec((1, GROUP_WIDTH), lambda i: (0, N_GROUPS + g))]
    args += [w_g, qk_gain, qk_gain]
    out_specs = pl.BlockSpec((3, 1, dil, rows, GROUP_WIDTH),
                             lambda i: (0, i // tiles_per_seq, 0, i % tiles_per_seq, 0))
    out_shape = jax.ShapeDtypeStruct((3, batch, dil, seq // dil, GROUP_WIDTH), BF16)
    if from_x:
        out_specs = (out_specs, row)
        out_shape = (out_shape, jax.ShapeDtypeStruct((t, d), BF16))
    return pl.pallas_call(
        functools.partial(_qkv_body, dilation=dil, from_x=from_x),
        grid=(t // QKV_TM,),
        in_specs=in_specs,
        out_specs=out_specs,
        out_shape=out_shape,
        scratch_shapes=[pltpu.VMEM((3 * HEADS_PER_GROUP, QKV_TM, HEAD_DIM), F32)],
        compiler_params=_params(("parallel",)),
        name=f"qkv_proj_g{g}",
    )(*args)


def _gate_body(h_ref, wu_ref, wb_ref, wc_ref, wa_ref, wv_ref, cw_ref, y_ref, sa_ref, sv_ref, tail_ref,
               *, tiles_per_seq):
    i, j = pl.program_id(0), pl.program_id(1)

    @pl.when(i == 0)
    def _():
        tail_ref[j] = jnp.zeros(tail_ref.shape[1:], F32)

    h = h_ref[...]
    dot = lambda w_ref: jnp.dot(h, w_ref[...], preferred_element_type=F32)
    z = dot(wc_ref) * dot(wu_ref)
    tail = jnp.where(i % tiles_per_seq == 0, 0.0, tail_ref[j])
    p1 = tail[SUBLANES - 1:SUBLANES, :]
    p2 = tail[SUBLANES - 2:SUBLANES - 1, :]
    rows = lax.broadcasted_iota(jnp.int32, z.shape, 0)
    zm1 = jnp.where(rows == 0, p1, pltpu.roll(z, 1, 0))
    zm2 = jnp.where(rows == 0, p2, jnp.where(rows == 1, p1, pltpu.roll(z, 2, 0)))
    conv = cw_ref[2:3, :] * z + cw_ref[1:2, :] * zm1 + cw_ref[0:1, :] * zm2
    tail_ref[j] = z[GATE_TM - SUBLANES:, :]
    y_ref[...] = (dot(wb_ref) * conv).astype(BF16)
    sa_ref[...] = jax.nn.sigmoid(dot(wa_ref)).astype(BF16)
    sv_ref[...] = jax.nn.sigmoid(dot(wv_ref)).astype(BF16)


def _gate_proj(h, w_in, conv_w, seq):
    t, d = h.shape
    assert seq % GATE_TM == 0
    wspec = lambda off: pl.BlockSpec((d, GATE_TN), lambda i, j: (0, off // GATE_TN + j))
    ospec = pl.BlockSpec((GATE_TM, GATE_TN), lambda i, j: (i, j))
    oshape = jax.ShapeDtypeStruct((t, d), BF16)
    return pl.pallas_call(
        functools.partial(_gate_body, tiles_per_seq=seq // GATE_TM),
        grid=(t // GATE_TM, d // GATE_TN),
        in_specs=[pl.BlockSpec((GATE_TM, d), lambda i, j: (i, 0)),
                  wspec(U_OFF), wspec(GB_OFF), wspec(GC_OFF), wspec(GA_OFF), wspec(GV_OFF),
                  pl.BlockSpec((CONV_WIDTH, GATE_TN), lambda i, j: (0, j))],
        out_specs=(ospec,) * 3,
        out_shape=(oshape,) * 3,
        scratch_shapes=[pltpu.VMEM((d // GATE_TN, SUBLANES, GATE_TN), F32)],
        compiler_params=_params(("arbitrary", "arbitrary")),
        name="gate_proj",
    )(h, w_in, w_in, w_in, w_in, w_in, conv_w)


def _attn_body(q_ref, kp_ref, kc_ref, vp_ref, vc_ref, o_ref, lse_ref, kbuf, vbuf, *, slopes, dilation, rows):
    c = pl.program_id(1)
    n_sub = rows // BLOCK
    kbuf[:, :BLOCK, :] = kp_ref[0, 0]
    kbuf[:, BLOCK:, :] = kc_ref[0, 0]
    vbuf[:, :BLOCK, :] = vp_ref[0, 0]
    vbuf[:, BLOCK:, :] = vc_ref[0, 0]

    qi = lax.broadcasted_iota(jnp.int32, (BLOCK, 2 * BLOCK), 0)
    ki = lax.broadcasted_iota(jnp.int32, (BLOCK, 2 * BLOCK), 1)
    dist = BLOCK + qi - ki
    band = (dist >= 0) & (dist <= BLOCK)
    in_cur = ki >= BLOCK
    ddist = (dilation * dist).astype(F32)

    def sub_block(it, carry):
        r = it // n_sub
        sb = it % n_sub
        row0 = pl.multiple_of(sb * BLOCK, BLOCK)
        valid = band & (in_cur | (sb > 0) | (c > 0))
        if dilation == 1:
            out_rows = pl.ds(row0, BLOCK)
        else:
            out_rows = pl.ds(row0 * dilation + r, BLOCK, stride=dilation)
        for hh in range(HEADS_PER_GROUP):
            q = q_ref[0, 0, r, pl.ds(row0, BLOCK), _head(hh)]
            k = kbuf[r, pl.ds(row0, 2 * BLOCK), _head(hh)]
            v = vbuf[r, pl.ds(row0, 2 * BLOCK), _head(hh)]
            s = lax.dot_general(q, k, (((1,), (1,)), ((), ())), preferred_element_type=F32)
            s = jnp.where(valid, s - slopes[hh] * ddist, NEG_INF)
            m = jnp.max(s, axis=-1, keepdims=True)
            p = jnp.exp(s - m)
            l = jnp.sum(p, axis=-1, keepdims=True)
            o_ref[hh, out_rows, :] = jnp.dot(p.astype(BF16), v, preferred_element_type=F32) / l
            lse_ref[hh, out_rows, :] = jnp.broadcast_to(m + jnp.log(l), (BLOCK, HEAD_DIM))
        return carry

    lax.fori_loop(0, dilation * n_sub, sub_block, 0, unroll=2)


def _attention_group(qkv, g, batch, seq):
    window, dil = DILATED_GROUPS[g]
    assert window // dil == BLOCK and seq % ATTN_CHUNK == 0 and ATTN_CHUNK % (dil * BLOCK) == 0
    rows = ATTN_CHUNK // dil
    chunks = seq // ATTN_CHUNK
    cur = lambda part: pl.BlockSpec((1, 1, dil, rows, GROUP_WIDTH), lambda b, c: (part, b, 0, c, 0))
    prev = lambda part: pl.BlockSpec((1, 1, dil, BLOCK, GROUP_WIDTH),
                                     lambda b, c: (part, b, 0, jnp.maximum(c * (rows // BLOCK) - 1, 0), 0))
    ospec = pl.BlockSpec((HEADS_PER_GROUP, ATTN_CHUNK, HEAD_DIM), lambda b, c: (0, b * chunks + c, 0))
    oshape = jax.ShapeDtypeStruct((HEADS_PER_GROUP, batch * seq, HEAD_DIM), F32)
    slopes = ALIBI_SLOPES[g * HEADS_PER_GROUP:(g + 1) * HEADS_PER_GROUP]
    return pl.pallas_call(
        functools.partial(_attn_body, slopes=slopes, dilation=dil, rows=rows),
        grid=(batch, chunks),
        in_specs=[cur(0), prev(1), cur(1), prev(2), cur(2)],
        out_specs=(ospec, ospec),
        out_shape=(oshape, oshape),
        scratch_shapes=[pltpu.VMEM((dil, BLOCK + rows, GROUP_WIDTH), BF16)] * 2,
        compiler_params=_params(("parallel", "arbitrary")),
        name=f"attention_g{g}",
    )(qkv, qkv, qkv, qkv, qkv)


def _mix_body(o0_ref, o1_ref, o2_ref, l0_ref, l1_ref, l2_ref, y_ref, sa_ref, sv_ref, x_ref,
              wa_ref, wc_ref, wo_ref, out_ref):
    branch_b = jnp.dot(y_ref[...], wc_ref[...], preferred_element_type=F32)

    heads = []
    for hh in range(HEADS_PER_GROUP):
        l0, l1, l2 = l0_ref[hh], l1_ref[hh], l2_ref[hh]
        m = jnp.maximum(jnp.maximum(l0, l1), l2)
        e0, e1, e2 = jnp.exp(l0 - m), jnp.exp(l1 - m), jnp.exp(l2 - m)
        heads.append(((e0 * o0_ref[hh] + e1 * o1_ref[hh] + e2 * o2_ref[hh]) / (e0 + e1 + e2)).astype(BF16))
    attn = jnp.concatenate(heads, axis=-1)
    branch_a = jnp.dot(attn, wa_ref[...], preferred_element_type=F32)

    merged = sa_ref[...].astype(F32) * branch_a + sv_ref[...].astype(F32) * branch_b
    out_ref[...] = x_ref[...] + jnp.dot(merged.astype(BF16), wo_ref[...], preferred_element_type=F32)


def _mix(os, lses, y, sa, sv, x, wa, wc, wo):
    t, d = x.shape
    row = pl.BlockSpec((MIX_TM, d), lambda i: (i, 0))
    head_rows = pl.BlockSpec((HEADS_PER_GROUP, MIX_TM, HEAD_DIM), lambda i: (0, i, 0))
    const = lambda a: pl.BlockSpec(a.shape, lambda i: (0, 0), pipeline_mode=pl.Buffered(1))
    return pl.pallas_call(
        _mix_body,
        grid=(t // MIX_TM,),
        in_specs=[head_rows] * 6 + [row, row, row, row, const(wa), const(wc), const(wo)],
        out_specs=row,
        out_shape=jax.ShapeDtypeStruct((t, d), F32),
        compiler_params=_params(("parallel",)),
        name="mix",
    )(*os, *lses, y, sa, sv, x, wa, wc, wo)


def kernel(x, ffn1_norm, ffn1_w_gate, ffn1_w_up, ffn1_w_down, mix_norm, w_in, q_norm, k_norm, conv_w,
           w_attn_out, w_conv_out, w_o, ffn2_norm, ffn2_w_gate, ffn2_w_up, ffn2_w_down):
    batch, seq, d = x.shape
    t = batch * seq
    xt = x.reshape(t, d)
    for layer in range(ffn1_norm.shape[0]):
        bf = lambda w: w[layer].astype(BF16)
        xt = _ffn(xt, ffn1_norm[layer], bf(ffn1_w_gate), bf(ffn1_w_up), bf(ffn1_w_down))
        w_in_b = bf(w_in)
        qk_gain = jnp.concatenate([q_norm[layer].reshape(-1) * (HEAD_DIM ** -0.5),
                                   k_norm[layer].reshape(-1)]).reshape(1, 2 * ATTN_WIDTH)
        os, lses = [], []
        h = None
        for g in range(N_GROUPS):
            w_g = jnp.concatenate([w_in_b[:, p * ATTN_WIDTH + g * GROUP_WIDTH:p * ATTN_WIDTH + (g + 1) * GROUP_WIDTH]
                                   for p in range(3)], axis=1)
            if h is None:
                qkv, h = _qkv_proj(xt, w_g, qk_gain, g, batch, seq, pre_norm_gain=mix_norm[layer])
            else:
                qkv = _qkv_proj(h, w_g, qk_gain, g, batch, seq)
            o_g, lse_g = _attention_group(qkv, g, batch, seq)
            os.append(o_g)
            lses.append(lse_g)
        y, sa, sv = _gate_proj(h, w_in_b, conv_w[layer], seq)
        xt = _mix(os, lses, y, sa, sv, xt, bf(w_attn_out), bf(w_conv_out), bf(w_o))
        xt = _ffn(xt, ffn2_norm[layer], bf(ffn2_w_gate), bf(ffn2_w_up), bf(ffn2_w_down))
    return xt.reshape(batch, seq, d)
```

```python
import functools

import jax
import jax.numpy as jnp
from jax import lax
from jax.experimental import pallas as pl
from jax.experimental.pallas import tpu as pltpu

F32 = jnp.float32
BF16 = jnp.bfloat16

D_MODEL = 2048
D_FF = 5632
HEAD_DIM = 128
DILATED_GROUPS = ((128, 1), (512, 4), (2048, 16))
N_GROUPS = len(DILATED_GROUPS)
HEADS_PER_GROUP = 4
N_ATTN_HEADS = N_GROUPS * HEADS_PER_GROUP
ATTN_WIDTH = N_ATTN_HEADS * HEAD_DIM
GROUP_WIDTH = HEADS_PER_GROUP * HEAD_DIM
QKV_WIDTH = 3 * ATTN_WIDTH
CONV_WIDTH = 3
BLOCK = 128
RMS_EPS = 1e-6
NEG_INF = -1e30
ALIBI_SLOPES = tuple(2.0 ** (-8.0 * (i + 1) / N_ATTN_HEADS) for i in range(N_ATTN_HEADS))
U_OFF = QKV_WIDTH
GB_OFF = U_OFF + D_MODEL
GC_OFF = GB_OFF + D_MODEL
GA_OFF = GC_OFF + D_MODEL
GV_OFF = GA_OFF + D_MODEL

VMEM_LIMIT_BYTES = 60 * 1024 * 1024
SUBLANES = 8

FFN_TM, FFN_TF = 1024, 512
QKV_TM, QKV_SUB = 1024, 256
GATE_TM, GATE_TN = 1024, 512
ATTN_CHUNK = 2048
MIX_TM = 256


def _params(sem):
    return pltpu.CompilerParams(dimension_semantics=sem, vmem_limit_bytes=VMEM_LIMIT_BYTES)


def _rms(x, gain):
    ms = jnp.mean(x * x, axis=-1, keepdims=True)
    return x * lax.rsqrt(ms + RMS_EPS) * gain


def _head(hh):
    return slice(hh * HEAD_DIM, (hh + 1) * HEAD_DIM)


def _ffn_body(x_ref, gain_ref, wg_ref, wu_ref, wd_ref, o_ref, xn_ref):
    def chunk(xn, base):
        g = jnp.dot(xn, wg_ref[...], preferred_element_type=F32)
        u = jnp.dot(xn, wu_ref[...], preferred_element_type=F32)
        a = (g * jax.nn.sigmoid(g) * (0.5 * u)).astype(BF16)
        o_ref[...] = base + jnp.dot(a, wd_ref[...], preferred_element_type=F32)

    @pl.when(pl.program_id(1) == 0)
    def _():
        x = x_ref[...]
        xn = _rms(x, gain_ref[...]).astype(BF16)
        xn_ref[...] = xn
        chunk(xn, x)

    @pl.when(pl.program_id(1) > 0)
    def _():
        chunk(xn_ref[...], o_ref[...])


def _ffn(x, gain, wg, wu, wd):
    t, d = x.shape
    row = pl.BlockSpec((FFN_TM, d), lambda i, j: (i, 0))
    return pl.pallas_call(
        _ffn_body,
        grid=(t // FFN_TM, D_FF // FFN_TF),
        in_specs=[row,
                  pl.BlockSpec((1, d), lambda i, j: (0, 0)),
                  pl.BlockSpec((d, FFN_TF), lambda i, j: (0, j)),
                  pl.BlockSpec((d, FFN_TF), lambda i, j: (0, j)),
                  pl.BlockSpec((FFN_TF, d), lambda i, j: (j, 0))],
        out_specs=row,
        out_shape=jax.ShapeDtypeStruct((t, d), F32),
        scratch_shapes=[pltpu.VMEM((FFN_TM, d), BF16)],
        compiler_params=_params(("parallel", "arbitrary")),
        name="ffn",
    )(x, gain.reshape(1, d), wg, wu, wd)


def _qkv_body(*refs, dilation, from_x):
    if from_x:
        x_ref, ng_ref, w_ref, qg_ref, kg_ref, o_ref, h_ref, scr_ref = refs
    else:
        h_ref, w_ref, qg_ref, kg_ref, o_ref, scr_ref = refs
    w = w_ref[...]
    gains = (qg_ref, kg_ref)
    sub = QKV_SUB // dilation
    for ch in range(QKV_TM // QKV_SUB):
        tok = pl.ds(ch * QKV_SUB, QKV_SUB)
        if from_x:
            h = _rms(x_ref[tok, :], ng_ref[...]).astype(BF16)
            h_ref[tok, :] = h
        else:
            h = h_ref[tok, :]
        acc = jnp.dot(h, w, preferred_element_type=F32)
        for part in range(3):
            for hh in range(HEADS_PER_GROUP):
                val = acc[:, part * GROUP_WIDTH + hh * HEAD_DIM:part * GROUP_WIDTH + (hh + 1) * HEAD_DIM]
                if part < 2:
                    val = _rms(val, gains[part][:, _head(hh)])
                if dilation == 1:
                    o_ref[part, 0, 0, tok, _head(hh)] = val.astype(BF16)
                else:
                    scr_ref[part * HEADS_PER_GROUP + hh, tok, :] = val
        if dilation > 1:
            for part in range(3):
                for r in range(dilation):
                    for hh in range(HEADS_PER_GROUP):
                        src = scr_ref[part * HEADS_PER_GROUP + hh, pl.ds(ch * QKV_SUB + r, sub, stride=dilation), :]
                        o_ref[part, 0, r, pl.ds(ch * sub, sub), _head(hh)] = src.astype(BF16)


def _qkv_proj(h, w_g, qk_gain, g, batch, seq, pre_norm_gain=None):
    _, dil = DILATED_GROUPS[g]
    t, d = h.shape
    tiles_per_seq = seq // QKV_TM
    rows = QKV_TM // dil
    from_x = pre_norm_gain is not None
    row = pl.BlockSpec((QKV_TM, d), lambda i: (i, 0))
    in_specs = [row]
    args = [h]
    if from_x:
        in_specs.append(pl.BlockSpec((1, d), lambda i: (0, 0)))
        args.append(pre_norm_gain.reshape(1, d))
    in_specs += [pl.BlockSpec((d, 3 * GROUP_WIDTH), lambda i: (0, 0), pipeline_mode=pl.Buffered(1)),
                 pl.BlockSpec((1, GROUP_WIDTH), lambda i: (0, g)),
                 pl.BlockSpec((1, GROUP_WIDTH), lambda i: (0, N_GROUPS + g))]
    args += [w_g, qk_gain, qk_gain]
    out_specs = pl.BlockSpec((3, 1, dil, rows, GROUP_WIDTH),
                             lambda i: (0, i // tiles_per_seq, 0, i % tiles_per_seq, 0))
    out_shape = jax.ShapeDtypeStruct((3, batch, dil, seq // dil, GROUP_WIDTH), BF16)
    if from_x:
        out_specs = (out_specs, row)
        out_shape = (out_shape, jax.ShapeDtypeStruct((t, d), BF16))
    return pl.pallas_call(
        functools.partial(_qkv_body, dilation=dil, from_x=from_x),
        grid=(t // QKV_TM,),
        in_specs=in_specs,
        out_specs=out_specs,
        out_shape=out_shape,
        scratch_shapes=[pltpu.VMEM((3 * HEADS_PER_GROUP, QKV_TM, HEAD_DIM), F32)],
        compiler_params=_params(("parallel",)),
        name=f"qkv_proj_g{g}",
    )(*args)


def _gate_body(h_ref, wu_ref, wb_ref, wc_ref, wa_ref, wv_ref, cw_ref, y_ref, sa_ref, sv_ref, tail_ref,
               *, tiles_per_seq):
    i, j = pl.program_id(0), pl.program_id(1)

    @pl.when(i == 0)
    def _():
        tail_ref[j] = jnp.zeros(tail_ref.shape[1:], F32)

    h = h_ref[...]
    dot = lambda w_ref: jnp.dot(h, w_ref[...], preferred_element_type=F32)
    z = dot(wc_ref) * dot(wu_ref)
    tail = jnp.where(i % tiles_per_seq == 0, 0.0, tail_ref[j])
    p1 = tail[SUBLANES - 1:SUBLANES, :]
    p2 = tail[SUBLANES - 2:SUBLANES - 1, :]
    rows = lax.broadcasted_iota(jnp.int32, z.shape, 0)
    zm1 = jnp.where(rows == 0, p1, pltpu.roll(z, 1, 0))
    zm2 = jnp.where(rows == 0, p2, jnp.where(rows == 1, p1, pltpu.roll(z, 2, 0)))
    conv = cw_ref[2:3, :] * z + cw_ref[1:2, :] * zm1 + cw_ref[0:1, :] * zm2
    tail_ref[j] = z[GATE_TM - SUBLANES:, :]
    y_ref[...] = (dot(wb_ref) * conv).astype(BF16)
    sa_ref[...] = jax.nn.sigmoid(dot(wa_ref)).astype(BF16)
    sv_ref[...] = jax.nn.sigmoid(dot(wv_ref)).astype(BF16)


def _gate_proj(h, w_in, conv_w, seq):
    t, d = h.shape
    assert seq % GATE_TM == 0
    wspec = lambda off: pl.BlockSpec((d, GATE_TN), lambda i, j: (0, off // GATE_TN + j))
    ospec = pl.BlockSpec((GATE_TM, GATE_TN), lambda i, j: (i, j))
    oshape = jax.ShapeDtypeStruct((t, d), BF16)
    return pl.pallas_call(
        functools.partial(_gate_body, tiles_per_seq=seq // GATE_TM),
        grid=(t // GATE_TM, d // GATE_TN),
        in_specs=[pl.BlockSpec((GATE_TM, d), lambda i, j: (i, 0)),
                  wspec(U_OFF), wspec(GB_OFF), wspec(GC_OFF), wspec(GA_OFF), wspec(GV_OFF),
                  pl.BlockSpec((CONV_WIDTH, GATE_TN), lambda i, j: (0, j))],
        out_specs=(ospec,) * 3,
        out_shape=(oshape,) * 3,
        scratch_shapes=[pltpu.VMEM((d // GATE_TN, SUBLANES, GATE_TN), F32)],
        compiler_params=_params(("arbitrary", "arbitrary")),
        name="gate_proj",
    )(h, w_in, w_in, w_in, w_in, w_in, conv_w)


def _attn_body(q_ref, kp_ref, kc_ref, vp_ref, vc_ref, o_ref, lse_ref, kbuf, vbuf, *, slopes, dilation, rows):
    c = pl.program_id(1)
    n_sub = rows // BLOCK
    kbuf[:, :BLOCK, :] = kp_ref[0, 0]
    kbuf[:, BLOCK:, :] = kc_ref[0, 0]
    vbuf[:, :BLOCK, :] = vp_ref[0, 0]
    vbuf[:, BLOCK:, :] = vc_ref[0, 0]

    qi = lax.broadcasted_iota(jnp.int32, (BLOCK, 2 * BLOCK), 0)
    ki = lax.broadcasted_iota(jnp.int32, (BLOCK, 2 * BLOCK), 1)
    dist = BLOCK + qi - ki
    band = (dist >= 0) & (dist <= BLOCK)
    in_cur = ki >= BLOCK
    ddist = (dilation * dist).astype(F32)

    def sub_block(it, carry):
        r = it // n_sub
        sb = it % n_sub
        row0 = pl.multiple_of(sb * BLOCK, BLOCK)
        valid = band & (in_cur | (sb > 0) | (c > 0))
        if dilation == 1:
            out_rows = pl.ds(row0, BLOCK)
        else:
            out_rows = pl.ds(row0 * dilation + r, BLOCK, stride=dilation)
        for hh in range(HEADS_PER_GROUP):
            q = q_ref[0, 0, r, pl.ds(row0, BLOCK), _head(hh)]
            k = kbuf[r, pl.ds(row0, 2 * BLOCK), _head(hh)]
            v = vbuf[r, pl.ds(row0, 2 * BLOCK), _head(hh)]
            s = lax.dot_general(q, k, (((1,), (1,)), ((), ())), preferred_element_type=F32)
            s = jnp.where(valid, s - slopes[hh] * ddist, NEG_INF)
            m = jnp.max(s, axis=-1, keepdims=True)
            p = jnp.exp(s - m)
            l = jnp.sum(p, axis=-1, keepdims=True)
            o_ref[hh, out_rows, :] = jnp.dot(p.astype(BF16), v, preferred_element_type=F32) / l
            lse_ref[hh, out_rows, :] = jnp.broadcast_to(m + jnp.log(l), (BLOCK, HEAD_DIM))
        return carry

    lax.fori_loop(0, dilation * n_sub, sub_block, 0, unroll=2)


def _attention_group(qkv, g, batch, seq):
    window, dil = DILATED_GROUPS[g]
    assert window // dil == BLOCK and seq % ATTN_CHUNK == 0 and ATTN_CHUNK % (dil * BLOCK) == 0
    rows = ATTN_CHUNK // dil
    chunks = seq // ATTN_CHUNK
    cur = lambda part: pl.BlockSpec((1, 1, dil, rows, GROUP_WIDTH), lambda b, c: (part, b, 0, c, 0))
    prev = lambda part: pl.BlockSpec((1, 1, dil, BLOCK, GROUP_WIDTH),
                                     lambda b, c: (part, b, 0, jnp.maximum(c * (rows // BLOCK) - 1, 0), 0))
    ospec = pl.BlockSpec((HEADS_PER_GROUP, ATTN_CHUNK, HEAD_DIM), lambda b, c: (0, b * chunks + c, 0))
    oshape = jax.ShapeDtypeStruct((HEADS_PER_GROUP, batch * seq, HEAD_DIM), F32)
    slopes = ALIBI_SLOPES[g * HEADS_PER_GROUP:(g + 1) * HEADS_PER_GROUP]
    return pl.pallas_call(
        functools.partial(_attn_body, slopes=slopes, dilation=dil, rows=rows),
        grid=(batch, chunks),
        in_specs=[cur(0), prev(1), cur(1), prev(2), cur(2)],
        out_specs=(ospec, ospec),
        out_shape=(oshape, oshape),
        scratch_shapes=[pltpu.VMEM((dil, BLOCK + rows, GROUP_WIDTH), BF16)] * 2,
        compiler_params=_params(("parallel", "arbitrary")),
        name=f"attention_g{g}",
    )(qkv, qkv, qkv, qkv, qkv)


def _mix_body(o0_ref, o1_ref, o2_ref, l0_ref, l1_ref, l2_ref, y_ref, sa_ref, sv_ref, x_ref,
              wa_ref, wc_ref, wo_ref, out_ref):
    branch_b = jnp.dot(y_ref[...], wc_ref[...], preferred_element_type=F32)

    heads = []
    for hh in range(HEADS_PER_GROUP):
        l0, l1, l2 = l0_ref[hh], l1_ref[hh], l2_ref[hh]
        m = jnp.maximum(jnp.maximum(l0, l1), l2)
        e0, e1, e2 = jnp.exp(l0 - m), jnp.exp(l1 - m), jnp.exp(l2 - m)
        heads.append(((e0 * o0_ref[hh] + e1 * o1_ref[hh] + e2 * o2_ref[hh]) / (e0 + e1 + e2)).astype(BF16))
    attn = jnp.concatenate(heads, axis=-1)
    branch_a = jnp.dot(attn, wa_ref[...], preferred_element_type=F32)

    merged = sa_ref[...].astype(F32) * branch_a + sv_ref[...].astype(F32) * branch_b
    out_ref[...] = x_ref[...] + jnp.dot(merged.astype(BF16), wo_ref[...], preferred_element_type=F32)


def _mix(os, lses, y, sa, sv, x, wa, wc, wo):
    t, d = x.shape
    row = pl.BlockSpec((MIX_TM, d), lambda i: (i, 0))
    head_rows = pl.BlockSpec((HEADS_PER_GROUP, MIX_TM, HEAD_DIM), lambda i: (0, i, 0))
    const = lambda a: pl.BlockSpec(a.shape, lambda i: (0, 0), pipeline_mode=pl.Buffered(1))
    return pl.pallas_call(
        _mix_body,
        grid=(t // MIX_TM,),
        in_specs=[head_rows] * 6 + [row, row, row, row, const(wa), const(wc), const(wo)],
        out_specs=row,
        out_shape=jax.ShapeDtypeStruct((t, d), F32),
        compiler_params=_params(("parallel",)),
        name="mix",
    )(*os, *lses, y, sa, sv, x, wa, wc, wo)


def kernel(x, ffn1_norm, ffn1_w_gate, ffn1_w_up, ffn1_w_down, mix_norm, w_in, q_norm, k_norm, conv_w,
           w_attn_out, w_conv_out, w_o, ffn2_norm, ffn2_w_gate, ffn2_w_up, ffn2_w_down):
    batch, seq, d = x.shape
    t = batch * seq
    xt = x.reshape(t, d)
    for layer in range(ffn1_norm.shape[0]):
        bf = lambda w: w[layer].astype(BF16)
        xt = _ffn(xt, ffn1_norm[layer], bf(ffn1_w_gate), bf(ffn1_w_up), bf(ffn1_w_down))
        w_in_b = bf(w_in)
        qk_gain = jnp.concatenate([q_norm[layer].reshape(-1) * (HEAD_DIM ** -0.5),
                                   k_norm[layer].reshape(-1)]).reshape(1, 2 * ATTN_WIDTH)
        os, lses = [], []
        h = None
        for g in range(N_GROUPS):
            w_g = jnp.concatenate([w_in_b[:, p * ATTN_WIDTH + g * GROUP_WIDTH:p * ATTN_WIDTH + (g + 1) * GROUP_WIDTH]
                                   for p in range(3)], axis=1)
            if h is None:
                qkv, h = _qkv_proj(xt, w_g, qk_gain, g, batch, seq, pre_norm_gain=mix_norm[layer])
            else:
                qkv = _qkv_proj(h, w_g, qk_gain, g, batch, seq)
            o_g, lse_g = _attention_group(qkv, g, batch, seq)
            os.append(o_g)
            lses.append(lse_g)
        y, sa, sv = _gate_proj(h, w_in_b, conv_w[layer], seq)
        xt = _mix(os, lses, y, sa, sv, xt, bf(w_attn_out), bf(w_conv_out), bf(w_o))
        xt = _ffn(xt, ffn2_norm[layer], bf(ffn2_w_gate), bf(ffn2_w_up), bf(ffn2_w_down))
    return xt.reshape(batch, seq, d)
```

```python
import functools

import jax
import jax.numpy as jnp
from jax import lax
from jax.experimental import pallas as pl
from jax.experimental.pallas import tpu as pltpu

F32 = jnp.float32
BF16 = jnp.bfloat16

D_MODEL = 2048
D_FF = 5632
HEAD_DIM = 128
DILATED_GROUPS = ((128, 1), (512, 4), (2048, 16))
N_GROUPS = len(DILATED_GROUPS)
HEADS_PER_GROUP = 4
N_ATTN_HEADS = N_GROUPS * HEADS_PER_GROUP
ATTN_WIDTH = N_ATTN_HEADS * HEAD_DIM
GROUP_WIDTH = HEADS_PER_GROUP * HEAD_DIM
QKV_WIDTH = 3 * ATTN_WIDTH
CONV_WIDTH = 3
BLOCK = 128
RMS_EPS = 1e-6
NEG_INF = -1e30
ALIBI_SLOPES = tuple(2.0 ** (-8.0 * (i + 1) / N_ATTN_HEADS) for i in range(N_ATTN_HEADS))
U_OFF = QKV_WIDTH
GB_OFF = U_OFF + D_MODEL
GC_OFF = GB_OFF + D_MODEL
GA_OFF = GC_OFF + D_MODEL
GV_OFF = GA_OFF + D_MODEL

VMEM_LIMIT_BYTES = 60 * 1024 * 1024
SUBLANES = 8
BF16_SUBLANES = 16

FFN_TM, FFN_TF = 1024, 512
QKV_TM, QKV_SUB = 1024, 256
GATE_TM, GATE_TN = 1024, 512
ATTN_CHUNK = 2048
MIX_TM = 256


def _params(sem):
    return pltpu.CompilerParams(dimension_semantics=sem, vmem_limit_bytes=VMEM_LIMIT_BYTES)


def _rms(x, gain):
    ms = jnp.mean(x * x, axis=-1, keepdims=True)
    return x * lax.rsqrt(ms + RMS_EPS) * gain


def _head(hh):
    return slice(hh * HEAD_DIM, (hh + 1) * HEAD_DIM)


def _rider_spec(w, grid):
    rows, cols = w.shape
    steps = grid[0] * grid[1]
    rb = next(r for r in range(BF16_SUBLANES, rows + 1, BF16_SUBLANES) if rows % r == 0 and rows // r <= steps)
    n_blocks = rows // rb
    repeat = steps // n_blocks
    return pl.BlockSpec((rb, cols), lambda i, j: (jnp.minimum((i * grid[1] + j) // repeat, n_blocks - 1), 0))


def _cast_riders(srcs, dsts):
    for src, dst in zip(srcs, dsts):
        dst[...] = src[...].astype(BF16)


def _rider_shapes(riders):
    return [jax.ShapeDtypeStruct(w.shape, BF16) for w in riders]


def _ffn_body(*refs, n_riders):
    x_ref, gain_ref, wg_ref, wu_ref, wd_ref = refs[:5]
    rider_src = refs[5:5 + n_riders]
    o_ref = refs[5 + n_riders]
    rider_dst = refs[6 + n_riders:6 + 2 * n_riders]
    xn_ref = refs[6 + 2 * n_riders]

    def chunk(xn, base):
        _cast_riders(rider_src, rider_dst)
        g = jnp.dot(xn, wg_ref[...], preferred_element_type=F32)
        u = jnp.dot(xn, wu_ref[...], preferred_element_type=F32)
        a = (g * jax.nn.sigmoid(g) * (0.5 * u)).astype(BF16)
        o_ref[...] = base + jnp.dot(a, wd_ref[...], preferred_element_type=F32)

    @pl.when(pl.program_id(1) == 0)
    def _():
        x = x_ref[...]
        xn = _rms(x, gain_ref[...]).astype(BF16)
        xn_ref[...] = xn
        chunk(xn, x)

    @pl.when(pl.program_id(1) > 0)
    def _():
        chunk(xn_ref[...], o_ref[...])


def _ffn(x, gain, wg, wu, wd, riders=()):
    t, d = x.shape
    grid = (t // FFN_TM, D_FF // FFN_TF)
    row = pl.BlockSpec((FFN_TM, d), lambda i, j: (i, 0))
    rider_specs = [_rider_spec(w, grid) for w in riders]
    out = pl.pallas_call(
        functools.partial(_ffn_body, n_riders=len(riders)),
        grid=grid,
        in_specs=[row,
                  pl.BlockSpec((1, d), lambda i, j: (0, 0)),
                  pl.BlockSpec((d, FFN_TF), lambda i, j: (0, j)),
                  pl.BlockSpec((d, FFN_TF), lambda i, j: (0, j)),
                  pl.BlockSpec((FFN_TF, d), lambda i, j: (j, 0))] + rider_specs,
        out_specs=[row] + rider_specs,
        out_shape=[jax.ShapeDtypeStruct((t, d), F32)] + _rider_shapes(riders),
        scratch_shapes=[pltpu.VMEM((FFN_TM, d), BF16)],
        compiler_params=_params(("arbitrary", "arbitrary")),
        name="ffn",
    )(x, gain.reshape(1, d), wg, wu, wd, *riders)
    return out[0], out[1:]


def _qkv_body(*refs, dilation, from_x):
    if from_x:
        x_ref, ng_ref, w_ref, qg_ref, kg_ref, o_ref, h_ref, scr_ref = refs
    else:
        h_ref, w_ref, qg_ref, kg_ref, o_ref, scr_ref = refs
    w = w_ref[...]
    gains = (qg_ref, kg_ref)
    sub = QKV_SUB // dilation
    for ch in range(QKV_TM // QKV_SUB):
        tok = pl.ds(ch * QKV_SUB, QKV_SUB)
        if from_x:
            h = _rms(x_ref[tok, :], ng_ref[...]).astype(BF16)
            h_ref[tok, :] = h
        else:
            h = h_ref[tok, :]
        acc = jnp.dot(h, w, preferred_element_type=F32)
        for part in range(3):
            for hh in range(HEADS_PER_GROUP):
                val = acc[:, part * GROUP_WIDTH + hh * HEAD_DIM:part * GROUP_WIDTH + (hh + 1) * HEAD_DIM]
                if part < 2:
                    val = _rms(val, gains[part][:, _head(hh)])
                if dilation == 1:
                    o_ref[part, 0, 0, tok, _head(hh)] = val.astype(BF16)
                else:
                    scr_ref[part * HEADS_PER_GROUP + hh, tok, :] = val
        if dilation > 1:
            for part in range(3):
                for r in range(dilation):
                    for hh in range(HEADS_PER_GROUP):
                        src = scr_ref[part * HEADS_PER_GROUP + hh, pl.ds(ch * QKV_SUB + r, sub, stride=dilation), :]
                        o_ref[part, 0, r, pl.ds(ch * sub, sub), _head(hh)] = src.astype(BF16)


def _qkv_proj(h, w_g, qk_gain, g, batch, seq, pre_norm_gain=None):
    _, dil = DILATED_GROUPS[g]
    t, d = h.shape
    tiles_per_seq = seq // QKV_TM
    rows = QKV_TM // dil
    from_x = pre_norm_gain is not None
    row = pl.BlockSpec((QKV_TM, d), lambda i: (i, 0))
    in_specs = [row]
    args = [h]
    if from_x:
        in_specs.append(pl.BlockSpec((1, d), lambda i: (0, 0)))
        args.append(pre_norm_gain.reshape(1, d))
    in_specs += [pl.BlockSpec((d, 3 * GROUP_WIDTH), lambda i: (0, 0), pipeline_mode=pl.Buffered(1)),
                 pl.BlockSpec((1, GROUP_WIDTH), lambda i: (0, g)),
                 pl.BlockSpec((1, GROUP_WIDTH), lambda i: (0, N_GROUPS + g))]
    args += [w_g, qk_gain, qk_gain]
    out_specs = pl.BlockSpec((3, 1, dil, rows, GROUP_WIDTH),
                             lambda i: (0, i // tiles_per_seq, 0, i % tiles_per_seq, 0))
    out_shape = jax.ShapeDtypeStruct((3, batch, dil, seq // dil, GROUP_WIDTH), BF16)
    if from_x:
        out_specs = (out_specs, row)
        out_shape = (out_shape, jax.ShapeDtypeStruct((t, d), BF16))
    return pl.pallas_call(
        functools.partial(_qkv_body, dilation=dil, from_x=from_x),
        grid=(t // QKV_TM,),
        in_specs=in_specs,
        out_specs=out_specs,
        out_shape=out_shape,
        scratch_shapes=[pltpu.VMEM((3 * HEADS_PER_GROUP, QKV_TM, HEAD_DIM), F32)],
        compiler_params=_params(("parallel",)),
        name=f"qkv_proj_g{g}",
    )(*args)


def _gate_body(*refs, tiles_per_seq, n_riders):
    h_ref, wu_ref, wb_ref, wc_ref, wa_ref, wv_ref, cw_ref = refs[:7]
    rider_src = refs[7:7 + n_riders]
    y_ref, sa_ref, sv_ref = refs[7 + n_riders:10 + n_riders]
    rider_dst = refs[10 + n_riders:10 + 2 * n_riders]
    tail_ref = refs[10 + 2 * n_riders]
    i, j = pl.program_id(0), pl.program_id(1)

    @pl.when(i == 0)
    def _():
        tail_ref[j] = jnp.zeros(tail_ref.shape[1:], F32)

    _cast_riders(rider_src, rider_dst)
    h = h_ref[...]
    dot = lambda w_ref: jnp.dot(h, w_ref[...], preferred_element_type=F32)
    z = dot(wc_ref) * dot(wu_ref)
    tail = jnp.where(i % tiles_per_seq == 0, 0.0, tail_ref[j])
    p1 = tail[SUBLANES - 1:SUBLANES, :]
    p2 = tail[SUBLANES - 2:SUBLANES - 1, :]
    rows = lax.broadcasted_iota(jnp.int32, z.shape, 0)
    zm1 = jnp.where(rows == 0, p1, pltpu.roll(z, 1, 0))
    zm2 = jnp.where(rows == 0, p2, jnp.where(rows == 1, p1, pltpu.roll(z, 2, 0)))
    conv = cw_ref[2:3, :] * z + cw_ref[1:2, :] * zm1 + cw_ref[0:1, :] * zm2
    tail_ref[j] = z[GATE_TM - SUBLANES:, :]
    y_ref[...] = (dot(wb_ref) * conv).astype(BF16)
    sa_ref[...] = jax.nn.sigmoid(dot(wa_ref)).astype(BF16)
    sv_ref[...] = jax.nn.sigmoid(dot(wv_ref)).astype(BF16)


def _gate_proj(h, w_in, conv_w, seq, riders=()):
    t, d = h.shape
    assert seq % GATE_TM == 0
    grid = (t // GATE_TM, d // GATE_TN)
    wspec = lambda off: pl.BlockSpec((d, GATE_TN), lambda i, j: (0, off // GATE_TN + j))
    ospec = pl.BlockSpec((GATE_TM, GATE_TN), lambda i, j: (i, j))
    oshape = jax.ShapeDtypeStruct((t, d), BF16)
    rider_specs = [_rider_spec(w, grid) for w in riders]
    out = pl.pallas_call(
        functools.partial(_gate_body, tiles_per_seq=seq // GATE_TM, n_riders=len(riders)),
        grid=grid,
        in_specs=[pl.BlockSpec((GATE_TM, d), lambda i, j: (i, 0)),
                  wspec(U_OFF), wspec(GB_OFF), wspec(GC_OFF), wspec(GA_OFF), wspec(GV_OFF),
                  pl.BlockSpec((CONV_WIDTH, GATE_TN), lambda i, j: (0, j))] + rider_specs,
        out_specs=[ospec] * 3 + rider_specs,
        out_shape=[oshape] * 3 + _rider_shapes(riders),
        scratch_shapes=[pltpu.VMEM((d // GATE_TN, SUBLANES, GATE_TN), F32)],
        compiler_params=_params(("arbitrary", "arbitrary")),
        name="gate_proj",
    )(h, w_in, w_in, w_in, w_in, w_in, conv_w, *riders)
    return out[:3], out[3:]


def _attn_body(q_ref, kp_ref, kc_ref, vp_ref, vc_ref, o_ref, lse_ref, kbuf, vbuf, *, slopes, dilation, rows):
    c = pl.program_id(1)
    n_sub = rows // BLOCK
    kbuf[:, :BLOCK, :] = kp_ref[0, 0]
    kbuf[:, BLOCK:, :] = kc_ref[0, 0]
    vbuf[:, :BLOCK, :] = vp_ref[0, 0]
    vbuf[:, BLOCK:, :] = vc_ref[0, 0]

    qi = lax.broadcasted_iota(jnp.int32, (BLOCK, 2 * BLOCK), 0)
    ki = lax.broadcasted_iota(jnp.int32, (BLOCK, 2 * BLOCK), 1)
    dist = BLOCK + qi - ki
    band = (dist >= 0) & (dist <= BLOCK)
    in_cur = ki >= BLOCK
    ddist = (dilation * dist).astype(F32)

    def sub_block(it, carry):
        r = it // n_sub
        sb = it % n_sub
        row0 = pl.multiple_of(sb * BLOCK, BLOCK)
        valid = band & (in_cur | (sb > 0) | (c > 0))
        if dilation == 1:
            out_rows = pl.ds(row0, BLOCK)
        else:
            out_rows = pl.ds(row0 * dilation + r, BLOCK, stride=dilation)
        for hh in range(HEADS_PER_GROUP):
            q = q_ref[0, 0, r, pl.ds(row0, BLOCK), _head(hh)]
            k = kbuf[r, pl.ds(row0, 2 * BLOCK), _head(hh)]
            v = vbuf[r, pl.ds(row0, 2 * BLOCK), _head(hh)]
            s = lax.dot_general(q, k, (((1,), (1,)), ((), ())), preferred_element_type=F32)
            s = jnp.where(valid, s - slopes[hh] * ddist, NEG_INF)
            m = jnp.max(s, axis=-1, keepdims=True)
            p = jnp.exp(s - m)
            l = jnp.sum(p, axis=-1, keepdims=True)
            o_ref[hh, out_rows, :] = jnp.dot(p.astype(BF16), v, preferred_element_type=F32) / l
            lse_ref[hh, out_rows, :] = jnp.broadcast_to(m + jnp.log(l), (BLOCK, HEAD_DIM))
        return carry

    lax.fori_loop(0, dilation * n_sub, sub_block, 0, unroll=2)


def _attention_group(qkv, g, batch, seq):
    window, dil = DILATED_GROUPS[g]
    assert window // dil == BLOCK and seq % ATTN_CHUNK == 0 and ATTN_CHUNK % (dil * BLOCK) == 0
    rows = ATTN_CHUNK // dil
    chunks = seq // ATTN_CHUNK
    cur = lambda part: pl.BlockSpec((1, 1, dil, rows, GROUP_WIDTH), lambda b, c: (part, b, 0, c, 0))
    prev = lambda part: pl.BlockSpec((1, 1, dil, BLOCK, GROUP_WIDTH),
                                     lambda b, c: (part, b, 0, jnp.maximum(c * (rows // BLOCK) - 1, 0), 0))
    ospec = pl.BlockSpec((HEADS_PER_GROUP, ATTN_CHUNK, HEAD_DIM), lambda b, c: (0, b * chunks + c, 0))
    oshape = jax.ShapeDtypeStruct((HEADS_PER_GROUP, batch * seq, HEAD_DIM), F32)
    slopes = ALIBI_SLOPES[g * HEADS_PER_GROUP:(g + 1) * HEADS_PER_GROUP]
    return pl.pallas_call(
        functools.partial(_attn_body, slopes=slopes, dilation=dil, rows=rows),
        grid=(batch, chunks),
        in_specs=[cur(0), prev(1), cur(1), prev(2), cur(2)],
        out_specs=(ospec, ospec),
        out_shape=(oshape, oshape),
        scratch_shapes=[pltpu.VMEM((dil, BLOCK + rows, GROUP_WIDTH), BF16)] * 2,
        compiler_params=_params(("parallel", "arbitrary")),
        name=f"attention_g{g}",
    )(qkv, qkv, qkv, qkv, qkv)


def _mix_body(o0_ref, o1_ref, o2_ref, l0_ref, l1_ref, l2_ref, y_ref, sa_ref, sv_ref, x_ref,
              wa_ref, wc_ref, wo_ref, out_ref):
    branch_b = jnp.dot(y_ref[...], wc_ref[...], preferred_element_type=F32)

    heads = []
    for hh in range(HEADS_PER_GROUP):
        l0, l1, l2 = l0_ref[hh], l1_ref[hh], l2_ref[hh]
        m = jnp.maximum(jnp.maximum(l0, l1), l2)
        e0, e1, e2 = jnp.exp(l0 - m), jnp.exp(l1 - m), jnp.exp(l2 - m)
        heads.append(((e0 * o0_ref[hh] + e1 * o1_ref[hh] + e2 * o2_ref[hh]) / (e0 + e1 + e2)).astype(BF16))
    attn = jnp.concatenate(heads, axis=-1)
    branch_a = jnp.dot(attn, wa_ref[...], preferred_element_type=F32)

    merged = sa_ref[...].astype(F32) * branch_a + sv_ref[...].astype(F32) * branch_b
    out_ref[...] = x_ref[...] + jnp.dot(merged.astype(BF16), wo_ref[...], preferred_element_type=F32)


def _mix(os, lses, y, sa, sv, x, wa, wc, wo):
    t, d = x.shape
    row = pl.BlockSpec((MIX_TM, d), lambda i: (i, 0))
    head_rows = pl.BlockSpec((HEADS_PER_GROUP, MIX_TM, HEAD_DIM), lambda i: (0, i, 0))
    const = lambda a: pl.BlockSpec(a.shape, lambda i: (0, 0), pipeline_mode=pl.Buffered(1))
    return pl.pallas_call(
        _mix_body,
        grid=(t // MIX_TM,),
        in_specs=[head_rows] * 6 + [row, row, row, row, const(wa), const(wc), const(wo)],
        out_specs=row,
        out_shape=jax.ShapeDtypeStruct((t, d), F32),
        compiler_params=_params(("parallel",)),
        name="mix",
    )(*os, *lses, y, sa, sv, x, wa, wc, wo)


def kernel(x, ffn1_norm, ffn1_w_gate, ffn1_w_up, ffn1_w_down, mix_norm, w_in, q_norm, k_norm, conv_w,
           w_attn_out, w_conv_out, w_o, ffn2_norm, ffn2_w_gate, ffn2_w_up, ffn2_w_down):
    batch, seq, d = x.shape
    t = batch * seq
    xt = x.reshape(t, d)
    for layer in range(ffn1_norm.shape[0]):
        bf = lambda w: w[layer].astype(BF16)
        xt, (w_in_b,) = _ffn(xt, ffn1_norm[layer], bf(ffn1_w_gate), bf(ffn1_w_up), bf(ffn1_w_down),
                             riders=[w_in[layer]])
        qk_gain = jnp.concatenate([q_norm[layer].reshape(-1) * (HEAD_DIM ** -0.5),
                                   k_norm[layer].reshape(-1)]).reshape(1, 2 * ATTN_WIDTH)
        os, lses = [], []
        h = None
        for g in range(N_GROUPS):
            w_g = jnp.concatenate([w_in_b[:, p * ATTN_WIDTH + g * GROUP_WIDTH:p * ATTN_WIDTH + (g + 1) * GROUP_WIDTH]
                                   for p in range(3)], axis=1)
            if h is None:
                qkv, h = _qkv_proj(xt, w_g, qk_gain, g, batch, seq, pre_norm_gain=mix_norm[layer])
            else:
                qkv = _qkv_proj(h, w_g, qk_gain, g, batch, seq)
            o_g, lse_g = _attention_group(qkv, g, batch, seq)
            os.append(o_g)
            lses.append(lse_g)
        later = (w_attn_out, w_conv_out, w_o, ffn2_w_gate, ffn2_w_up, ffn2_w_down)
        (y, sa, sv), (wa_b, wc_b, wo_b, wg2_b, wu2_b, wd2_b) = _gate_proj(
            h, w_in_b, conv_w[layer], seq, riders=[w[layer] for w in later])
        xt = _mix(os, lses, y, sa, sv, xt, wa_b, wc_b, wo_b)
        xt, _ = _ffn(xt, ffn2_norm[layer], wg2_b, wu2_b, wd2_b)
    return xt.reshape(batch, seq, d)
```

```python
import functools

import jax
import jax.numpy as jnp
from jax import lax
from jax.experimental import pallas as pl
from jax.experimental.pallas import tpu as pltpu

F32 = jnp.float32
BF16 = jnp.bfloat16

D_MODEL = 2048
D_FF = 5632
HEAD_DIM = 128
DILATED_GROUPS = ((128, 1), (512, 4), (2048, 16))
N_GROUPS = len(DILATED_GROUPS)
HEADS_PER_GROUP = 4
N_ATTN_HEADS = N_GROUPS * HEADS_PER_GROUP
ATTN_WIDTH = N_ATTN_HEADS * HEAD_DIM
GROUP_WIDTH = HEADS_PER_GROUP * HEAD_DIM
QKV_WIDTH = 3 * ATTN_WIDTH
CONV_WIDTH = 3
BLOCK = 128
LSE_LANES = HEAD_DIM // HEADS_PER_GROUP
RMS_EPS = 1e-6
NEG_INF = -1e30
ALIBI_SLOPES = tuple(2.0 ** (-8.0 * (i + 1) / N_ATTN_HEADS) for i in range(N_ATTN_HEADS))
U_OFF = QKV_WIDTH
GB_OFF = U_OFF + D_MODEL
GC_OFF = GB_OFF + D_MODEL
GA_OFF = GC_OFF + D_MODEL
GV_OFF = GA_OFF + D_MODEL

VMEM_LIMIT_BYTES = 60 * 1024 * 1024
SUBLANES = 8
BF16_SUBLANES = 16

FFN_TM, FFN_TF = 1024, 512
QKV_TM, QKV_SUB = 1024, 256
GATE_TM, GATE_TN, GATE_SUB = 1024, 512, 256
ATTN_CHUNK = 2048
MIX_TM = 256


def _params(sem):
    return pltpu.CompilerParams(dimension_semantics=sem, vmem_limit_bytes=VMEM_LIMIT_BYTES)


def _rms(x, gain):
    ms = jnp.mean(x * x, axis=-1, keepdims=True)
    return x * lax.rsqrt(ms + RMS_EPS) * gain


def _head(hh):
    return slice(hh * HEAD_DIM, (hh + 1) * HEAD_DIM)


def _rider_spec(w, grid):
    rows, cols = w.shape
    steps = grid[0] * grid[1]
    rb = next(r for r in range(BF16_SUBLANES, rows + 1, BF16_SUBLANES) if rows % r == 0 and rows // r <= steps)
    n_blocks = rows // rb
    repeat = steps // n_blocks
    return pl.BlockSpec((rb, cols), lambda i, j: (jnp.minimum((i * grid[1] + j) // repeat, n_blocks - 1), 0))


def _cast_riders(srcs, dsts):
    for src, dst in zip(srcs, dsts):
        dst[...] = src[...].astype(BF16)


def _rider_shapes(riders):
    return [jax.ShapeDtypeStruct(w.shape, BF16) for w in riders]


def _ffn_body(*refs, n_riders):
    x_ref, gain_ref, wg_ref, wu_ref, wd_ref = refs[:5]
    rider_src = refs[5:5 + n_riders]
    o_ref = refs[5 + n_riders]
    rider_dst = refs[6 + n_riders:6 + 2 * n_riders]
    xn_ref = refs[6 + 2 * n_riders]

    def chunk(xn, base):
        _cast_riders(rider_src, rider_dst)
        g = jnp.dot(xn, wg_ref[...], preferred_element_type=F32)
        u = jnp.dot(xn, wu_ref[...], preferred_element_type=F32)
        a = (g * jax.nn.sigmoid(g) * (0.5 * u)).astype(BF16)
        o_ref[...] = base + jnp.dot(a, wd_ref[...], preferred_element_type=F32)

    @pl.when(pl.program_id(1) == 0)
    def _():
        x = x_ref[...]
        xn = _rms(x, gain_ref[...]).astype(BF16)
        xn_ref[...] = xn
        chunk(xn, x)

    @pl.when(pl.program_id(1) > 0)
    def _():
        chunk(xn_ref[...], o_ref[...])


def _ffn(x, gain, wg, wu, wd, riders=()):
    t, d = x.shape
    grid = (t // FFN_TM, D_FF // FFN_TF)
    row = pl.BlockSpec((FFN_TM, d), lambda i, j: (i, 0))
    rider_specs = [_rider_spec(w, grid) for w in riders]
    out = pl.pallas_call(
        functools.partial(_ffn_body, n_riders=len(riders)),
        grid=grid,
        in_specs=[row,
                  pl.BlockSpec((1, d), lambda i, j: (0, 0)),
                  pl.BlockSpec((d, FFN_TF), lambda i, j: (0, j)),
                  pl.BlockSpec((d, FFN_TF), lambda i, j: (0, j)),
                  pl.BlockSpec((FFN_TF, d), lambda i, j: (j, 0))] + rider_specs,
        out_specs=[row] + rider_specs,
        out_shape=[jax.ShapeDtypeStruct((t, d), F32)] + _rider_shapes(riders),
        scratch_shapes=[pltpu.VMEM((FFN_TM, d), BF16)],
        compiler_params=_params(("arbitrary", "arbitrary")),
        name="ffn",
    )(x, gain.reshape(1, d), wg, wu, wd, *riders)
    return out[0], out[1:]


def _qkv_body(*refs, dilation, from_x):
    if from_x:
        x_ref, ng_ref, w_ref, qg_ref, kg_ref, o_ref, h_ref, scr_ref = refs
    else:
        h_ref, w_ref, qg_ref, kg_ref, o_ref, scr_ref = refs
    w = w_ref[...]
    gains = (qg_ref, kg_ref)
    sub = QKV_SUB // dilation
    for ch in range(QKV_TM // QKV_SUB):
        tok = pl.ds(ch * QKV_SUB, QKV_SUB)
        if from_x:
            h = _rms(x_ref[tok, :], ng_ref[...]).astype(BF16)
            h_ref[tok, :] = h
        else:
            h = h_ref[tok, :]
        acc = jnp.dot(h, w, preferred_element_type=F32)
        for part in range(3):
            for hh in range(HEADS_PER_GROUP):
                val = acc[:, part * GROUP_WIDTH + hh * HEAD_DIM:part * GROUP_WIDTH + (hh + 1) * HEAD_DIM]
                if part < 2:
                    val = _rms(val, gains[part][:, _head(hh)])
                if dilation == 1:
                    o_ref[part, 0, 0, tok, _head(hh)] = val.astype(BF16)
                else:
                    scr_ref[part * HEADS_PER_GROUP + hh, tok, :] = val
        if dilation > 1:
            for part in range(3):
                for r in range(dilation):
                    for hh in range(HEADS_PER_GROUP):
                        src = scr_ref[part * HEADS_PER_GROUP + hh, pl.ds(ch * QKV_SUB + r, sub, stride=dilation), :]
                        o_ref[part, 0, r, pl.ds(ch * sub, sub), _head(hh)] = src.astype(BF16)


def _qkv_proj(h, w_g, qk_gain, g, batch, seq, pre_norm_gain=None):
    _, dil = DILATED_GROUPS[g]
    t, d = h.shape
    tiles_per_seq = seq // QKV_TM
    rows = QKV_TM // dil
    from_x = pre_norm_gain is not None
    row = pl.BlockSpec((QKV_TM, d), lambda i: (i, 0))
    in_specs = [row]
    args = [h]
    if from_x:
        in_specs.append(pl.BlockSpec((1, d), lambda i: (0, 0)))
        args.append(pre_norm_gain.reshape(1, d))
    in_specs += [pl.BlockSpec((d, 3 * GROUP_WIDTH), lambda i: (0, 0), pipeline_mode=pl.Buffered(1)),
                 pl.BlockSpec((1, GROUP_WIDTH), lambda i: (0, g)),
                 pl.BlockSpec((1, GROUP_WIDTH), lambda i: (0, N_GROUPS + g))]
    args += [w_g, qk_gain, qk_gain]
    out_specs = pl.BlockSpec((3, 1, dil, rows, GROUP_WIDTH),
                             lambda i: (0, i // tiles_per_seq, 0, i % tiles_per_seq, 0))
    out_shape = jax.ShapeDtypeStruct((3, batch, dil, seq // dil, GROUP_WIDTH), BF16)
    if from_x:
        out_specs = (out_specs, row)
        out_shape = (out_shape, jax.ShapeDtypeStruct((t, d), BF16))
    return pl.pallas_call(
        functools.partial(_qkv_body, dilation=dil, from_x=from_x),
        grid=(t // QKV_TM,),
        in_specs=in_specs,
        out_specs=out_specs,
        out_shape=out_shape,
        scratch_shapes=[pltpu.VMEM((3 * HEADS_PER_GROUP, QKV_TM, HEAD_DIM), F32)],
        compiler_params=_params(("parallel",)),
        name=f"qkv_proj_g{g}",
    )(*args)


def _gate_body(*refs, tiles_per_seq, n_riders):
    h_ref, wu_ref, wb_ref, wc_ref, wa_ref, wv_ref, cw_ref = refs[:7]
    rider_src = refs[7:7 + n_riders]
    y_ref, sa_ref, sv_ref = refs[7 + n_riders:10 + n_riders]
    rider_dst = refs[10 + n_riders:10 + 2 * n_riders]
    tail_ref = refs[10 + 2 * n_riders]
    i, j = pl.program_id(0), pl.program_id(1)

    @pl.when(i == 0)
    def _():
        tail_ref[j] = jnp.zeros(tail_ref.shape[1:], F32)

    tail = jnp.where(i % tiles_per_seq == 0, 0.0, tail_ref[j])
    rows = lax.broadcasted_iota(jnp.int32, (GATE_SUB, GATE_TN), 0)
    for ch in range(GATE_TM // GATE_SUB):
        tok = pl.ds(ch * GATE_SUB, GATE_SUB)
        h = h_ref[tok, :]
        dot = lambda w_ref: jnp.dot(h, w_ref[...], preferred_element_type=F32)
        z = dot(wc_ref) * dot(wu_ref)
        p1 = tail[SUBLANES - 1:SUBLANES, :]
        p2 = tail[SUBLANES - 2:SUBLANES - 1, :]
        zm1 = jnp.where(rows == 0, p1, pltpu.roll(z, 1, 0))
        zm2 = jnp.where(rows == 0, p2, jnp.where(rows == 1, p1, pltpu.roll(z, 2, 0)))
        conv = cw_ref[2:3, :] * z + cw_ref[1:2, :] * zm1 + cw_ref[0:1, :] * zm2
        tail = z[GATE_SUB - SUBLANES:, :]
        y_ref[tok, :] = (dot(wb_ref) * conv).astype(BF16)
        sa_ref[tok, :] = jax.nn.sigmoid(dot(wa_ref)).astype(BF16)
        sv_ref[tok, :] = jax.nn.sigmoid(dot(wv_ref)).astype(BF16)
    tail_ref[j] = tail
    _cast_riders(rider_src, rider_dst)


def _gate_proj(h, w_in, conv_w, seq, riders=()):
    t, d = h.shape
    assert seq % GATE_TM == 0
    grid = (t // GATE_TM, d // GATE_TN)
    wspec = lambda off: pl.BlockSpec((d, GATE_TN), lambda i, j: (0, off // GATE_TN + j))
    ospec = pl.BlockSpec((GATE_TM, GATE_TN), lambda i, j: (i, j))
    oshape = jax.ShapeDtypeStruct((t, d), BF16)
    rider_specs = [_rider_spec(w, grid) for w in riders]
    out = pl.pallas_call(
        functools.partial(_gate_body, tiles_per_seq=seq // GATE_TM, n_riders=len(riders)),
        grid=grid,
        in_specs=[pl.BlockSpec((GATE_TM, d), lambda i, j: (i, 0)),
                  wspec(U_OFF), wspec(GB_OFF), wspec(GC_OFF), wspec(GA_OFF), wspec(GV_OFF),
                  pl.BlockSpec((CONV_WIDTH, GATE_TN), lambda i, j: (0, j))] + rider_specs,
        out_specs=[ospec] * 3 + rider_specs,
        out_shape=[oshape] * 3 + _rider_shapes(riders),
        scratch_shapes=[pltpu.VMEM((d // GATE_TN, SUBLANES, GATE_TN), F32)],
        compiler_params=_params(("arbitrary", "arbitrary")),
        name="gate_proj",
    )(h, w_in, w_in, w_in, w_in, w_in, conv_w, *riders)
    return out[:3], out[3:]


def _attn_body(q_ref, kp_ref, kc_ref, vp_ref, vc_ref, o_ref, lse_ref, kbuf, vbuf, *, slopes, dilation, rows):
    c = pl.program_id(1)
    n_sub = rows // BLOCK
    kbuf[:, :BLOCK, :] = kp_ref[0, 0]
    kbuf[:, BLOCK:, :] = kc_ref[0, 0]
    vbuf[:, :BLOCK, :] = vp_ref[0, 0]
    vbuf[:, BLOCK:, :] = vc_ref[0, 0]

    qi = lax.broadcasted_iota(jnp.int32, (BLOCK, 2 * BLOCK), 0)
    ki = lax.broadcasted_iota(jnp.int32, (BLOCK, 2 * BLOCK), 1)
    dist = BLOCK + qi - ki
    band = (dist >= 0) & (dist <= BLOCK)
    in_cur = ki >= BLOCK
    ddist = (dilation * dist).astype(F32)
    lane = lax.broadcasted_iota(jnp.int32, (BLOCK, HEAD_DIM), 1)

    def sub_block(it, carry):
        r = it // n_sub
        sb = it % n_sub
        row0 = pl.multiple_of(sb * BLOCK, BLOCK)
        valid = band & (in_cur | (sb > 0) | (c > 0))
        if dilation == 1:
            out_rows = pl.ds(row0, BLOCK)
        else:
            out_rows = pl.ds(row0 * dilation + r, BLOCK, stride=dilation)
        lse = None
        for hh in range(HEADS_PER_GROUP):
            q = q_ref[0, 0, r, pl.ds(row0, BLOCK), _head(hh)]
            k = kbuf[r, pl.ds(row0, 2 * BLOCK), _head(hh)]
            v = vbuf[r, pl.ds(row0, 2 * BLOCK), _head(hh)]
            s = lax.dot_general(q, k, (((1,), (1,)), ((), ())), preferred_element_type=F32)
            s = jnp.where(valid, s - slopes[hh] * ddist, NEG_INF)
            m = jnp.max(s, axis=-1, keepdims=True)
            p = jnp.exp(s - m)
            l = jnp.sum(p, axis=-1, keepdims=True)
            o_ref[hh, out_rows, :] = jnp.dot(p.astype(BF16), v, preferred_element_type=F32) / l
            lse_h = jnp.broadcast_to(m + jnp.log(l), (BLOCK, HEAD_DIM))
            lse = lse_h if lse is None else jnp.where(lane >= hh * LSE_LANES, lse_h, lse)
        lse_ref[out_rows, :] = lse
        return carry

    lax.fori_loop(0, dilation * n_sub, sub_block, 0, unroll=8)


def _attention_group(qkv, g, batch, seq):
    window, dil = DILATED_GROUPS[g]
    assert window // dil == BLOCK and seq % ATTN_CHUNK == 0 and ATTN_CHUNK % (dil * BLOCK) == 0
    rows = ATTN_CHUNK // dil
    chunks = seq // ATTN_CHUNK
    cur = lambda part: pl.BlockSpec((1, 1, dil, rows, GROUP_WIDTH), lambda b, c: (part, b, 0, c, 0))
    prev = lambda part: pl.BlockSpec((1, 1, dil, BLOCK, GROUP_WIDTH),
                                     lambda b, c: (part, b, 0, jnp.maximum(c * (rows // BLOCK) - 1, 0), 0))
    ospec = pl.BlockSpec((HEADS_PER_GROUP, ATTN_CHUNK, HEAD_DIM), lambda b, c: (0, b * chunks + c, 0))
    oshape = jax.ShapeDtypeStruct((HEADS_PER_GROUP, batch * seq, HEAD_DIM), F32)
    lspec = pl.BlockSpec((ATTN_CHUNK, HEAD_DIM), lambda b, c: (b * chunks + c, 0))
    lshape = jax.ShapeDtypeStruct((batch * seq, HEAD_DIM), F32)
    slopes = ALIBI_SLOPES[g * HEADS_PER_GROUP:(g + 1) * HEADS_PER_GROUP]
    return pl.pallas_call(
        functools.partial(_attn_body, slopes=slopes, dilation=dil, rows=rows),
        grid=(batch, chunks),
        in_specs=[cur(0), prev(1), cur(1), prev(2), cur(2)],
        out_specs=(ospec, lspec),
        out_shape=(oshape, lshape),
        scratch_shapes=[pltpu.VMEM((dil, BLOCK + rows, GROUP_WIDTH), BF16)] * 2,
        compiler_params=_params(("parallel", "arbitrary")),
        name=f"attention_g{g}",
    )(qkv, qkv, qkv, qkv, qkv)


def _mix_body(o0_ref, o1_ref, o2_ref, l0_ref, l1_ref, l2_ref, y_ref, sa_ref, sv_ref, x_ref,
              wa_ref, wc_ref, wo_ref, out_ref):
    branch_b = jnp.dot(y_ref[...], wc_ref[...], preferred_element_type=F32)

    l0, l1, l2 = l0_ref[...], l1_ref[...], l2_ref[...]
    m = jnp.maximum(jnp.maximum(l0, l1), l2)
    e0, e1, e2 = jnp.exp(l0 - m), jnp.exp(l1 - m), jnp.exp(l2 - m)
    inv = 1.0 / (e0 + e1 + e2)
    alphas = (e0 * inv, e1 * inv, e2 * inv)
    heads = []
    for hh in range(HEADS_PER_GROUP):
        a0, a1, a2 = (jnp.broadcast_to(a[:, hh * LSE_LANES:hh * LSE_LANES + 1], (MIX_TM, HEAD_DIM)) for a in alphas)
        heads.append((a0 * o0_ref[hh] + a1 * o1_ref[hh] + a2 * o2_ref[hh]).astype(BF16))
    attn = jnp.concatenate(heads, axis=-1)
    branch_a = jnp.dot(attn, wa_ref[...], preferred_element_type=F32)

    merged = sa_ref[...].astype(F32) * branch_a + sv_ref[...].astype(F32) * branch_b
    out_ref[...] = x_ref[...] + jnp.dot(merged.astype(BF16), wo_ref[...], preferred_element_type=F32)


def _mix(os, lses, y, sa, sv, x, wa, wc, wo):
    t, d = x.shape
    row = pl.BlockSpec((MIX_TM, d), lambda i: (i, 0))
    head_rows = pl.BlockSpec((HEADS_PER_GROUP, MIX_TM, HEAD_DIM), lambda i: (0, i, 0))
    lse_rows = pl.BlockSpec((MIX_TM, HEAD_DIM), lambda i: (i, 0))
    const = lambda a: pl.BlockSpec(a.shape, lambda i: (0, 0), pipeline_mode=pl.Buffered(1))
    return pl.pallas_call(
        _mix_body,
        grid=(t // MIX_TM,),
        in_specs=[head_rows] * 3 + [lse_rows] * 3 + [row, row, row, row, const(wa), const(wc), const(wo)],
        out_specs=row,
        out_shape=jax.ShapeDtypeStruct((t, d), F32),
        compiler_params=_params(("parallel",)),
        name="mix",
    )(*os, *lses, y, sa, sv, x, wa, wc, wo)


def kernel(x, ffn1_norm, ffn1_w_gate, ffn1_w_up, ffn1_w_down, mix_norm, w_in, q_norm, k_norm, conv_w,
           w_attn_out, w_conv_out, w_o, ffn2_norm, ffn2_w_gate, ffn2_w_up, ffn2_w_down):
    batch, seq, d = x.shape
    t = batch * seq
    xt = x.reshape(t, d)
    for layer in range(ffn1_norm.shape[0]):
        bf = lambda w: w[layer].astype(BF16)
        xt, (w_in_b,) = _ffn(xt, ffn1_norm[layer], bf(ffn1_w_gate), bf(ffn1_w_up), bf(ffn1_w_down),
                             riders=[w_in[layer]])
        qk_gain = jnp.concatenate([q_norm[layer].reshape(-1) * (HEAD_DIM ** -0.5),
                                   k_norm[layer].reshape(-1)]).reshape(1, 2 * ATTN_WIDTH)
        os, lses = [], []
        h = None
        for g in range(N_GROUPS):
            w_g = jnp.concatenate([w_in_b[:, p * ATTN_WIDTH + g * GROUP_WIDTH:p * ATTN_WIDTH + (g + 1) * GROUP_WIDTH]
                                   for p in range(3)], axis=1)
            if h is None:
                qkv, h = _qkv_proj(xt, w_g, qk_gain, g, batch, seq, pre_norm_gain=mix_norm[layer])
            else:
                qkv = _qkv_proj(h, w_g, qk_gain, g, batch, seq)
            o_g, lse_g = _attention_group(qkv, g, batch, seq)
            os.append(o_g)
            lses.append(lse_g)
        later = (w_attn_out, w_conv_out, w_o, ffn2_w_gate, ffn2_w_up, ffn2_w_down)
        (y, sa, sv), (wa_b, wc_b, wo_b, wg2_b, wu2_b, wd2_b) = _gate_proj(
            h, w_in_b, conv_w[layer], seq, riders=[w[layer] for w in later])
        xt = _mix(os, lses, y, sa, sv, xt, wa_b, wc_b, wo_b)
        xt, _ = _ffn(xt, ffn2_norm[layer], wg2_b, wu2_b, wd2_b)
    return xt.reshape(batch, seq, d)
```

```python
import functools

import jax
import jax.numpy as jnp
from jax import lax
from jax.experimental import pallas as pl
from jax.experimental.pallas import tpu as pltpu

F32 = jnp.float32
BF16 = jnp.bfloat16

D_MODEL = 2048
D_FF = 5632
HEAD_DIM = 128
DILATED_GROUPS = ((128, 1), (512, 4), (2048, 16))
N_GROUPS = len(DILATED_GROUPS)
HEADS_PER_GROUP = 4
N_ATTN_HEADS = N_GROUPS * HEADS_PER_GROUP
ATTN_WIDTH = N_ATTN_HEADS * HEAD_DIM
GROUP_WIDTH = HEADS_PER_GROUP * HEAD_DIM
QKV_WIDTH = 3 * ATTN_WIDTH
CONV_WIDTH = 3
BLOCK = 128
LSE_LANES = HEAD_DIM // HEADS_PER_GROUP
RMS_EPS = 1e-6
NEG_INF = -1e30
ALIBI_SLOPES = tuple(2.0 ** (-8.0 * (i + 1) / N_ATTN_HEADS) for i in range(N_ATTN_HEADS))
U_OFF = QKV_WIDTH
GB_OFF = U_OFF + D_MODEL
GC_OFF = GB_OFF + D_MODEL
GA_OFF = GC_OFF + D_MODEL
GV_OFF = GA_OFF + D_MODEL

VMEM_LIMIT_BYTES = 60 * 1024 * 1024
SUBLANES = 8
FAST_STRIDE = 4
BF16_SUBLANES = 16

FFN_TM, FFN_TF = 1024, 512
QKV_TM, QKV_SUB = 1024, 256
GATE_TM, GATE_TN, GATE_SUB = 1024, 512, 256
ATTN_CHUNK = 2048
MIX_TM = 256


def _params(sem):
    return pltpu.CompilerParams(dimension_semantics=sem, vmem_limit_bytes=VMEM_LIMIT_BYTES)


def _rms(x, gain):
    ms = jnp.mean(x * x, axis=-1, keepdims=True)
    return x * lax.rsqrt(ms + RMS_EPS) * gain


def _head(hh):
    return slice(hh * HEAD_DIM, (hh + 1) * HEAD_DIM)


def _rider_spec(w, grid):
    rows, cols = w.shape
    steps = grid[0] * grid[1]
    rb = next(r for r in range(BF16_SUBLANES, rows + 1, BF16_SUBLANES) if rows % r == 0 and rows // r <= steps)
    n_blocks = rows // rb
    repeat = steps // n_blocks
    return pl.BlockSpec((rb, cols), lambda i, j: (jnp.minimum((i * grid[1] + j) // repeat, n_blocks - 1), 0))


def _cast_riders(srcs, dsts):
    for src, dst in zip(srcs, dsts):
        dst[...] = src[...].astype(BF16)


def _rider_shapes(riders):
    return [jax.ShapeDtypeStruct(w.shape, BF16) for w in riders]


def _ffn_body(*refs, n_riders):
    x_ref, gain_ref, wg_ref, wu_ref, wd_ref = refs[:5]
    rider_src = refs[5:5 + n_riders]
    o_ref = refs[5 + n_riders]
    rider_dst = refs[6 + n_riders:6 + 2 * n_riders]
    xn_ref = refs[6 + 2 * n_riders]

    def chunk(xn, base):
        _cast_riders(rider_src, rider_dst)
        g = jnp.dot(xn, wg_ref[...], preferred_element_type=F32)
        u = jnp.dot(xn, wu_ref[...], preferred_element_type=F32)
        a = (g * jax.nn.sigmoid(g) * (0.5 * u)).astype(BF16)
        o_ref[...] = base + jnp.dot(a, wd_ref[...], preferred_element_type=F32)

    @pl.when(pl.program_id(1) == 0)
    def _():
        x = x_ref[...]
        xn = _rms(x, gain_ref[...]).astype(BF16)
        xn_ref[...] = xn
        chunk(xn, x)

    @pl.when(pl.program_id(1) > 0)
    def _():
        chunk(xn_ref[...], o_ref[...])


def _ffn(x, gain, wg, wu, wd, riders=()):
    t, d = x.shape
    grid = (t // FFN_TM, D_FF // FFN_TF)
    row = pl.BlockSpec((FFN_TM, d), lambda i, j: (i, 0))
    rider_specs = [_rider_spec(w, grid) for w in riders]
    out = pl.pallas_call(
        functools.partial(_ffn_body, n_riders=len(riders)),
        grid=grid,
        in_specs=[row,
                  pl.BlockSpec((1, d), lambda i, j: (0, 0)),
                  pl.BlockSpec((d, FFN_TF), lambda i, j: (0, j)),
                  pl.BlockSpec((d, FFN_TF), lambda i, j: (0, j)),
                  pl.BlockSpec((FFN_TF, d), lambda i, j: (j, 0))] + rider_specs,
        out_specs=[row] + rider_specs,
        out_shape=[jax.ShapeDtypeStruct((t, d), F32)] + _rider_shapes(riders),
        scratch_shapes=[pltpu.VMEM((FFN_TM, d), BF16)],
        compiler_params=_params(("arbitrary", "arbitrary")),
        name="ffn",
    )(x, gain.reshape(1, d), wg, wu, wd, *riders)
    return out[0], out[1:]


def _qkv_body(*refs, dilation, from_x):
    if from_x:
        x_ref, ng_ref, w_ref, qg_ref, kg_ref, o_ref, h_ref, scr_ref, tmp_ref = refs
    else:
        h_ref, w_ref, qg_ref, kg_ref, o_ref, scr_ref, tmp_ref = refs
    w = w_ref[...]
    gains = (qg_ref, kg_ref)
    sub = QKV_SUB // dilation
    for ch in range(QKV_TM // QKV_SUB):
        tok = pl.ds(ch * QKV_SUB, QKV_SUB)
        if from_x:
            h = _rms(x_ref[tok, :], ng_ref[...]).astype(BF16)
            h_ref[tok, :] = h
        else:
            h = h_ref[tok, :]
        acc = jnp.dot(h, w, preferred_element_type=F32)
        for part in range(3):
            for hh in range(HEADS_PER_GROUP):
                val = acc[:, part * GROUP_WIDTH + hh * HEAD_DIM:part * GROUP_WIDTH + (hh + 1) * HEAD_DIM]
                if part < 2:
                    val = _rms(val, gains[part][:, _head(hh)])
                if dilation == 1:
                    o_ref[part, 0, 0, tok, _head(hh)] = val.astype(BF16)
                else:
                    scr_ref[part * HEADS_PER_GROUP + hh, tok, :] = val
        if 1 < dilation <= FAST_STRIDE:
            for slab in range(3 * HEADS_PER_GROUP):
                part, hh = divmod(slab, HEADS_PER_GROUP)
                for r in range(dilation):
                    src = scr_ref[slab, pl.ds(ch * QKV_SUB + r, sub, stride=dilation), :]
                    o_ref[part, 0, r, pl.ds(ch * sub, sub), _head(hh)] = src.astype(BF16)
        elif dilation > FAST_STRIDE:
            outer = dilation // FAST_STRIDE
            quarter = QKV_SUB // FAST_STRIDE
            for slab in range(3 * HEADS_PER_GROUP):
                part, hh = divmod(slab, HEADS_PER_GROUP)
                for a in range(FAST_STRIDE):
                    tmp_ref[slab, pl.ds(ch * QKV_SUB + a * quarter, quarter), :] = (
                        scr_ref[slab, pl.ds(ch * QKV_SUB + a, quarter, stride=FAST_STRIDE), :])
                for a in range(FAST_STRIDE):
                    for q in range(outer):
                        src = tmp_ref[slab, pl.ds(ch * QKV_SUB + a * quarter + q, sub, stride=outer), :]
                        o_ref[part, 0, a + FAST_STRIDE * q, pl.ds(ch * sub, sub), _head(hh)] = src.astype(BF16)


def _qkv_proj(h, w_g, qk_gain, g, batch, seq, pre_norm_gain=None):
    _, dil = DILATED_GROUPS[g]
    t, d = h.shape
    tiles_per_seq = seq // QKV_TM
    rows = QKV_TM // dil
    from_x = pre_norm_gain is not None
    row = pl.BlockSpec((QKV_TM, d), lambda i: (i, 0))
    in_specs = [row]
    args = [h]
    if from_x:
        in_specs.append(pl.BlockSpec((1, d), lambda i: (0, 0)))
        args.append(pre_norm_gain.reshape(1, d))
    in_specs += [pl.BlockSpec((d, 3 * GROUP_WIDTH), lambda i: (0, 0), pipeline_mode=pl.Buffered(1)),
                 pl.BlockSpec((1, GROUP_WIDTH), lambda i: (0, g)),
                 pl.BlockSpec((1, GROUP_WIDTH), lambda i: (0, N_GROUPS + g))]
    args += [w_g, qk_gain, qk_gain]
    out_specs = pl.BlockSpec((3, 1, dil, rows, GROUP_WIDTH),
                             lambda i: (0, i // tiles_per_seq, 0, i % tiles_per_seq, 0))
    out_shape = jax.ShapeDtypeStruct((3, batch, dil, seq // dil, GROUP_WIDTH), BF16)
    if from_x:
        out_specs = (out_specs, row)
        out_shape = (out_shape, jax.ShapeDtypeStruct((t, d), BF16))
    return pl.pallas_call(
        functools.partial(_qkv_body, dilation=dil, from_x=from_x),
        grid=(t // QKV_TM,),
        in_specs=in_specs,
        out_specs=out_specs,
        out_shape=out_shape,
        scratch_shapes=[pltpu.VMEM((3 * HEADS_PER_GROUP, QKV_TM if use else SUBLANES, HEAD_DIM), F32)
                        for use in (dil > 1, dil > FAST_STRIDE)],
        compiler_params=_params(("parallel",)),
        name=f"qkv_proj_g{g}",
    )(*args)


def _gate_body(*refs, tiles_per_seq, n_riders):
    h_ref, wu_ref, wb_ref, wc_ref, wa_ref, wv_ref, cw_ref = refs[:7]
    rider_src = refs[7:7 + n_riders]
    y_ref, sa_ref, sv_ref = refs[7 + n_riders:10 + n_riders]
    rider_dst = refs[10 + n_riders:10 + 2 * n_riders]
    tail_ref = refs[10 + 2 * n_riders]
    i, j = pl.program_id(0), pl.program_id(1)

    @pl.when(i == 0)
    def _():
        tail_ref[j] = jnp.zeros(tail_ref.shape[1:], F32)

    tail = jnp.where(i % tiles_per_seq == 0, 0.0, tail_ref[j])
    rows = lax.broadcasted_iota(jnp.int32, (GATE_SUB, GATE_TN), 0)
    for ch in range(GATE_TM // GATE_SUB):
        tok = pl.ds(ch * GATE_SUB, GATE_SUB)
        h = h_ref[tok, :]
        dot = lambda w_ref: jnp.dot(h, w_ref[...], preferred_element_type=F32)
        z = dot(wc_ref) * dot(wu_ref)
        p1 = tail[SUBLANES - 1:SUBLANES, :]
        p2 = tail[SUBLANES - 2:SUBLANES - 1, :]
        zm1 = jnp.where(rows == 0, p1, pltpu.roll(z, 1, 0))
        zm2 = jnp.where(rows == 0, p2, jnp.where(rows == 1, p1, pltpu.roll(z, 2, 0)))
        conv = cw_ref[2:3, :] * z + cw_ref[1:2, :] * zm1 + cw_ref[0:1, :] * zm2
        tail = z[GATE_SUB - SUBLANES:, :]
        y_ref[tok, :] = (dot(wb_ref) * conv).astype(BF16)
        sa_ref[tok, :] = jax.nn.sigmoid(dot(wa_ref)).astype(BF16)
        sv_ref[tok, :] = jax.nn.sigmoid(dot(wv_ref)).astype(BF16)
    tail_ref[j] = tail
    _cast_riders(rider_src, rider_dst)


def _gate_proj(h, w_in, conv_w, seq, riders=()):
    t, d = h.shape
    assert seq % GATE_TM == 0
    grid = (t // GATE_TM, d // GATE_TN)
    wspec = lambda off: pl.BlockSpec((d, GATE_TN), lambda i, j: (0, off // GATE_TN + j))
    ospec = pl.BlockSpec((GATE_TM, GATE_TN), lambda i, j: (i, j))
    oshape = jax.ShapeDtypeStruct((t, d), BF16)
    rider_specs = [_rider_spec(w, grid) for w in riders]
    out = pl.pallas_call(
        functools.partial(_gate_body, tiles_per_seq=seq // GATE_TM, n_riders=len(riders)),
        grid=grid,
        in_specs=[pl.BlockSpec((GATE_TM, d), lambda i, j: (i, 0)),
                  wspec(U_OFF), wspec(GB_OFF), wspec(GC_OFF), wspec(GA_OFF), wspec(GV_OFF),
                  pl.BlockSpec((CONV_WIDTH, GATE_TN), lambda i, j: (0, j))] + rider_specs,
        out_specs=[ospec] * 3 + rider_specs,
        out_shape=[oshape] * 3 + _rider_shapes(riders),
        scratch_shapes=[pltpu.VMEM((d // GATE_TN, SUBLANES, GATE_TN), F32)],
        compiler_params=_params(("arbitrary", "arbitrary")),
        name="gate_proj",
    )(h, w_in, w_in, w_in, w_in, w_in, conv_w, *riders)
    return out[:3], out[3:]


def _attn_body(q_ref, kp_ref, kc_ref, vp_ref, vc_ref, o_ref, lse_ref, kbuf, vbuf, *, slopes, dilation, rows):
    c = pl.program_id(1)
    n_sub = rows // BLOCK
    kbuf[:, :BLOCK, :] = kp_ref[0, 0]
    kbuf[:, BLOCK:, :] = kc_ref[0, 0]
    vbuf[:, :BLOCK, :] = vp_ref[0, 0]
    vbuf[:, BLOCK:, :] = vc_ref[0, 0]

    qi = lax.broadcasted_iota(jnp.int32, (BLOCK, 2 * BLOCK), 0)
    ki = lax.broadcasted_iota(jnp.int32, (BLOCK, 2 * BLOCK), 1)
    dist = BLOCK + qi - ki
    band = (dist >= 0) & (dist <= BLOCK)
    in_cur = ki >= BLOCK
    ddist = (dilation * dist).astype(F32)
    lane = lax.broadcasted_iota(jnp.int32, (BLOCK, HEAD_DIM), 1)

    def sub_block(it, carry):
        r = it // n_sub
        sb = it % n_sub
        row0 = pl.multiple_of(sb * BLOCK, BLOCK)
        valid = band & (in_cur | (sb > 0) | (c > 0))
        if dilation == 1:
            out_rows = pl.ds(row0, BLOCK)
        else:
            out_rows = pl.ds(row0 * dilation + r, BLOCK, stride=dilation)
        lse = None
        for hh in range(HEADS_PER_GROUP):
            q = q_ref[0, 0, r, pl.ds(row0, BLOCK), _head(hh)]
            k = kbuf[r, pl.ds(row0, 2 * BLOCK), _head(hh)]
            v = vbuf[r, pl.ds(row0, 2 * BLOCK), _head(hh)]
            s = lax.dot_general(q, k, (((1,), (1,)), ((), ())), preferred_element_type=F32)
            s = jnp.where(valid, s - slopes[hh] * ddist, NEG_INF)
            m = jnp.max(s, axis=-1, keepdims=True)
            p = jnp.exp(s - m)
            l = jnp.sum(p, axis=-1, keepdims=True)
            o_ref[hh, out_rows, :] = jnp.dot(p.astype(BF16), v, preferred_element_type=F32) / l
            lse_h = jnp.broadcast_to(m + jnp.log(l), (BLOCK, HEAD_DIM))
            lse = lse_h if lse is None else jnp.where(lane >= hh * LSE_LANES, lse_h, lse)
        lse_ref[out_rows, :] = lse
        return carry

    lax.fori_loop(0, dilation * n_sub, sub_block, 0, unroll=8)


def _attention_group(qkv, g, batch, seq):
    window, dil = DILATED_GROUPS[g]
    assert window // dil == BLOCK and seq % ATTN_CHUNK == 0 and ATTN_CHUNK % (dil * BLOCK) == 0
    rows = ATTN_CHUNK // dil
    chunks = seq // ATTN_CHUNK
    cur = lambda part: pl.BlockSpec((1, 1, dil, rows, GROUP_WIDTH), lambda b, c: (part, b, 0, c, 0))
    prev = lambda part: pl.BlockSpec((1, 1, dil, BLOCK, GROUP_WIDTH),
                                     lambda b, c: (part, b, 0, jnp.maximum(c * (rows // BLOCK) - 1, 0), 0))
    ospec = pl.BlockSpec((HEADS_PER_GROUP, ATTN_CHUNK, HEAD_DIM), lambda b, c: (0, b * chunks + c, 0))
    oshape = jax.ShapeDtypeStruct((HEADS_PER_GROUP, batch * seq, HEAD_DIM), F32)
    lspec = pl.BlockSpec((ATTN_CHUNK, HEAD_DIM), lambda b, c: (b * chunks + c, 0))
    lshape = jax.ShapeDtypeStruct((batch * seq, HEAD_DIM), F32)
    slopes = ALIBI_SLOPES[g * HEADS_PER_GROUP:(g + 1) * HEADS_PER_GROUP]
    return pl.pallas_call(
        functools.partial(_attn_body, slopes=slopes, dilation=dil, rows=rows),
        grid=(batch, chunks),
        in_specs=[cur(0), prev(1), cur(1), prev(2), cur(2)],
        out_specs=(ospec, lspec),
        out_shape=(oshape, lshape),
        scratch_shapes=[pltpu.VMEM((dil, BLOCK + rows, GROUP_WIDTH), BF16)] * 2,
        compiler_params=_params(("parallel", "arbitrary")),
        name=f"attention_g{g}",
    )(qkv, qkv, qkv, qkv, qkv)


def _mix_body(o0_ref, o1_ref, o2_ref, l0_ref, l1_ref, l2_ref, y_ref, sa_ref, sv_ref, x_ref,
              wa_ref, wc_ref, wo_ref, out_ref):
    branch_b = jnp.dot(y_ref[...], wc_ref[...], preferred_element_type=F32)

    l0, l1, l2 = l0_ref[...], l1_ref[...], l2_ref[...]
    m = jnp.maximum(jnp.maximum(l0, l1), l2)
    e0, e1, e2 = jnp.exp(l0 - m), jnp.exp(l1 - m), jnp.exp(l2 - m)
    inv = 1.0 / (e0 + e1 + e2)
    alphas = (e0 * inv, e1 * inv, e2 * inv)
    heads = []
    for hh in range(HEADS_PER_GROUP):
        a0, a1, a2 = (jnp.broadcast_to(a[:, hh * LSE_LANES:hh * LSE_LANES + 1], (MIX_TM, HEAD_DIM)) for a in alphas)
        heads.append((a0 * o0_ref[hh] + a1 * o1_ref[hh] + a2 * o2_ref[hh]).astype(BF16))
    attn = jnp.concatenate(heads, axis=-1)
    branch_a = jnp.dot(attn, wa_ref[...], preferred_element_type=F32)

    merged = sa_ref[...].astype(F32) * branch_a + sv_ref[...].astype(F32) * branch_b
    out_ref[...] = x_ref[...] + jnp.dot(merged.astype(BF16), wo_ref[...], preferred_element_type=F32)


def _mix(os, lses, y, sa, sv, x, wa, wc, wo):
    t, d = x.shape
    row = pl.BlockSpec((MIX_TM, d), lambda i: (i, 0))
    head_rows = pl.BlockSpec((HEADS_PER_GROUP, MIX_TM, HEAD_DIM), lambda i: (0, i, 0))
    lse_rows = pl.BlockSpec((MIX_TM, HEAD_DIM), lambda i: (i, 0))
    const = lambda a: pl.BlockSpec(a.shape, lambda i: (0, 0), pipeline_mode=pl.Buffered(1))
    return pl.pallas_call(
        _mix_body,
        grid=(t // MIX_TM,),
        in_specs=[head_rows] * 3 + [lse_rows] * 3 + [row, row, row, row, const(wa), const(wc), const(wo)],
        out_specs=row,
        out_shape=jax.ShapeDtypeStruct((t, d), F32),
        compiler_params=_params(("parallel",)),
        name="mix",
    )(*os, *lses, y, sa, sv, x, wa, wc, wo)


def kernel(x, ffn1_norm, ffn1_w_gate, ffn1_w_up, ffn1_w_down, mix_norm, w_in, q_norm, k_norm, conv_w,
           w_attn_out, w_conv_out, w_o, ffn2_norm, ffn2_w_gate, ffn2_w_up, ffn2_w_down):
    batch, seq, d = x.shape
    t = batch * seq
    xt = x.reshape(t, d)
    for layer in range(ffn1_norm.shape[0]):
        bf = lambda w: w[layer].astype(BF16)
        xt, (w_in_b,) = _ffn(xt, ffn1_norm[layer], bf(ffn1_w_gate), bf(ffn1_w_up), bf(ffn1_w_down),
                             riders=[w_in[layer]])
        qk_gain = jnp.concatenate([q_norm[layer].reshape(-1) * (HEAD_DIM ** -0.5),
                                   k_norm[layer].reshape(-1)]).reshape(1, 2 * ATTN_WIDTH)
        os, lses = [], []
        h = None
        for g in range(N_GROUPS):
            w_g = jnp.concatenate([w_in_b[:, p * ATTN_WIDTH + g * GROUP_WIDTH:p * ATTN_WIDTH + (g + 1) * GROUP_WIDTH]
                                   for p in range(3)], axis=1)
            if h is None:
                qkv, h = _qkv_proj(xt, w_g, qk_gain, g, batch, seq, pre_norm_gain=mix_norm[layer])
            else:
                qkv = _qkv_proj(h, w_g, qk_gain, g, batch, seq)
            o_g, lse_g = _attention_group(qkv, g, batch, seq)
            os.append(o_g)
            lses.append(lse_g)
        later = (w_attn_out, w_conv_out, w_o, ffn2_w_gate, ffn2_w_up, ffn2_w_down)
        (y, sa, sv), (wa_b, wc_b, wo_b, wg2_b, wu2_b, wd2_b) = _gate_proj(
            h, w_in_b, conv_w[layer], seq, riders=[w[layer] for w in later])
        xt = _mix(os, lses, y, sa, sv, xt, wa_b, wc_b, wo_b)
        xt, _ = _ffn(xt, ffn2_norm[layer], wg2_b, wu2_b, wd2_b)
    return xt.reshape(batch, seq, d)
```

```python
import functools

import jax
import jax.numpy as jnp
from jax import lax
from jax.experimental import pallas as pl
from jax.experimental.pallas import tpu as pltpu

F32 = jnp.float32
BF16 = jnp.bfloat16

D_MODEL = 2048
D_FF = 5632
HEAD_DIM = 128
DILATED_GROUPS = ((128, 1), (512, 4), (2048, 16))
N_GROUPS = len(DILATED_GROUPS)
HEADS_PER_GROUP = 4
N_ATTN_HEADS = N_GROUPS * HEADS_PER_GROUP
ATTN_WIDTH = N_ATTN_HEADS * HEAD_DIM
GROUP_WIDTH = HEADS_PER_GROUP * HEAD_DIM
QKV_WIDTH = 3 * ATTN_WIDTH
CONV_WIDTH = 3
BLOCK = 128
LSE_LANES = HEAD_DIM // HEADS_PER_GROUP
RMS_EPS = 1e-6
NEG_INF = -1e30
ALIBI_SLOPES = tuple(2.0 ** (-8.0 * (i + 1) / N_ATTN_HEADS) for i in range(N_ATTN_HEADS))
U_OFF = QKV_WIDTH
GB_OFF = U_OFF + D_MODEL
GC_OFF = GB_OFF + D_MODEL
GA_OFF = GC_OFF + D_MODEL
GV_OFF = GA_OFF + D_MODEL

VMEM_LIMIT_BYTES = 60 * 1024 * 1024
SUBLANES = 8
FAST_STRIDE = 4
BF16_SUBLANES = 16

FFN_TM, FFN_TF = 1024, 512
QKV_TM, QKV_SUB = 1024, 256
GATE_TM, GATE_TN, GATE_SUB = 1024, 512, 256
ATTN_CHUNK = 2048
MIX_TM = 512


def _params(sem):
    return pltpu.CompilerParams(dimension_semantics=sem, vmem_limit_bytes=VMEM_LIMIT_BYTES)


def _rms(x, gain):
    ms = jnp.mean(x * x, axis=-1, keepdims=True)
    return x * lax.rsqrt(ms + RMS_EPS) * gain


def _head(hh):
    return slice(hh * HEAD_DIM, (hh + 1) * HEAD_DIM)


def _rider_spec(w, grid):
    rows, cols = w.shape
    steps = grid[0] * grid[1]
    rb = next(r for r in range(BF16_SUBLANES, rows + 1, BF16_SUBLANES) if rows % r == 0 and rows // r <= steps)
    n_blocks = rows // rb
    repeat = steps // n_blocks
    return pl.BlockSpec((rb, cols), lambda i, j: (jnp.minimum((i * grid[1] + j) // repeat, n_blocks - 1), 0))


def _cast_riders(srcs, dsts):
    for src, dst in zip(srcs, dsts):
        dst[...] = src[...].astype(BF16)


def _rider_shapes(riders):
    return [jax.ShapeDtypeStruct(w.shape, BF16) for w in riders]


def _ffn_body(*refs, n_riders):
    x_ref, gain_ref, wg_ref, wu_ref, wd_ref = refs[:5]
    rider_src = refs[5:5 + n_riders]
    o_ref = refs[5 + n_riders]
    rider_dst = refs[6 + n_riders:6 + 2 * n_riders]
    xn_ref = refs[6 + 2 * n_riders]

    def chunk(xn, base):
        _cast_riders(rider_src, rider_dst)
        g = jnp.dot(xn, wg_ref[...], preferred_element_type=F32)
        u = jnp.dot(xn, wu_ref[...], preferred_element_type=F32)
        a = (g * jax.nn.sigmoid(g) * (0.5 * u)).astype(BF16)
        o_ref[...] = base + jnp.dot(a, wd_ref[...], preferred_element_type=F32)

    @pl.when(pl.program_id(1) == 0)
    def _():
        x = x_ref[...]
        xn = _rms(x, gain_ref[...]).astype(BF16)
        xn_ref[...] = xn
        chunk(xn, x)

    @pl.when(pl.program_id(1) > 0)
    def _():
        chunk(xn_ref[...], o_ref[...])


def _ffn(x, gain, wg, wu, wd, riders=()):
    t, d = x.shape
    grid = (t // FFN_TM, D_FF // FFN_TF)
    row = pl.BlockSpec((FFN_TM, d), lambda i, j: (i, 0))
    rider_specs = [_rider_spec(w, grid) for w in riders]
    out = pl.pallas_call(
        functools.partial(_ffn_body, n_riders=len(riders)),
        grid=grid,
        in_specs=[row,
                  pl.BlockSpec((1, d), lambda i, j: (0, 0)),
                  pl.BlockSpec((d, FFN_TF), lambda i, j: (0, j)),
                  pl.BlockSpec((d, FFN_TF), lambda i, j: (0, j)),
                  pl.BlockSpec((FFN_TF, d), lambda i, j: (j, 0))] + rider_specs,
        out_specs=[row] + rider_specs,
        out_shape=[jax.ShapeDtypeStruct((t, d), F32)] + _rider_shapes(riders),
        scratch_shapes=[pltpu.VMEM((FFN_TM, d), BF16)],
        compiler_params=_params(("arbitrary", "arbitrary")),
        name="ffn",
    )(x, gain.reshape(1, d), wg, wu, wd, *riders)
    return out[0], out[1:]


def _qkv_body(*refs, dilation, from_x):
    if from_x:
        x_ref, ng_ref, w_ref, qg_ref, kg_ref, o_ref, h_ref, scr_ref, tmp_ref = refs
    else:
        h_ref, w_ref, qg_ref, kg_ref, o_ref, scr_ref, tmp_ref = refs
    w = w_ref[...]
    gains = (qg_ref, kg_ref)
    sub = QKV_SUB // dilation
    for ch in range(QKV_TM // QKV_SUB):
        tok = pl.ds(ch * QKV_SUB, QKV_SUB)
        if from_x:
            h = _rms(x_ref[tok, :], ng_ref[...]).astype(BF16)
            h_ref[tok, :] = h
        else:
            h = h_ref[tok, :]
        acc = jnp.dot(h, w, preferred_element_type=F32)
        for part in range(3):
            for hh in range(HEADS_PER_GROUP):
                val = acc[:, part * GROUP_WIDTH + hh * HEAD_DIM:part * GROUP_WIDTH + (hh + 1) * HEAD_DIM]
                if part < 2:
                    val = _rms(val, gains[part][:, _head(hh)])
                if dilation == 1:
                    o_ref[part, 0, 0, tok, _head(hh)] = val.astype(BF16)
                else:
                    scr_ref[part * HEADS_PER_GROUP + hh, tok, :] = val
        if 1 < dilation <= FAST_STRIDE:
            for slab in range(3 * HEADS_PER_GROUP):
                part, hh = divmod(slab, HEADS_PER_GROUP)
                for r in range(dilation):
                    src = scr_ref[slab, pl.ds(ch * QKV_SUB + r, sub, stride=dilation), :]
                    o_ref[part, 0, r, pl.ds(ch * sub, sub), _head(hh)] = src.astype(BF16)
        elif dilation > FAST_STRIDE:
            outer = dilation // FAST_STRIDE
            quarter = QKV_SUB // FAST_STRIDE
            for slab in range(3 * HEADS_PER_GROUP):
                part, hh = divmod(slab, HEADS_PER_GROUP)
                for a in range(FAST_STRIDE):
                    tmp_ref[slab, pl.ds(ch * QKV_SUB + a * quarter, quarter), :] = (
                        scr_ref[slab, pl.ds(ch * QKV_SUB + a, quarter, stride=FAST_STRIDE), :])
                for a in range(FAST_STRIDE):
                    for q in range(outer):
                        src = tmp_ref[slab, pl.ds(ch * QKV_SUB + a * quarter + q, sub, stride=outer), :]
                        o_ref[part, 0, a + FAST_STRIDE * q, pl.ds(ch * sub, sub), _head(hh)] = src.astype(BF16)


def _qkv_proj(h, w_g, qk_gain, g, batch, seq, pre_norm_gain=None):
    _, dil = DILATED_GROUPS[g]
    t, d = h.shape
    tiles_per_seq = seq // QKV_TM
    rows = QKV_TM // dil
    from_x = pre_norm_gain is not None
    row = pl.BlockSpec((QKV_TM, d), lambda i: (i, 0))
    in_specs = [row]
    args = [h]
    if from_x:
        in_specs.append(pl.BlockSpec((1, d), lambda i: (0, 0)))
        args.append(pre_norm_gain.reshape(1, d))
    in_specs += [pl.BlockSpec((d, 3 * GROUP_WIDTH), lambda i: (0, 0), pipeline_mode=pl.Buffered(1)),
                 pl.BlockSpec((1, GROUP_WIDTH), lambda i: (0, g)),
                 pl.BlockSpec((1, GROUP_WIDTH), lambda i: (0, N_GROUPS + g))]
    args += [w_g, qk_gain, qk_gain]
    out_specs = pl.BlockSpec((3, 1, dil, rows, GROUP_WIDTH),
                             lambda i: (0, i // tiles_per_seq, 0, i % tiles_per_seq, 0))
    out_shape = jax.ShapeDtypeStruct((3, batch, dil, seq // dil, GROUP_WIDTH), BF16)
    if from_x:
        out_specs = (out_specs, row)
        out_shape = (out_shape, jax.ShapeDtypeStruct((t, d), BF16))
    return pl.pallas_call(
        functools.partial(_qkv_body, dilation=dil, from_x=from_x),
        grid=(t // QKV_TM,),
        in_specs=in_specs,
        out_specs=out_specs,
        out_shape=out_shape,
        scratch_shapes=[pltpu.VMEM((3 * HEADS_PER_GROUP, QKV_TM if use else SUBLANES, HEAD_DIM), F32)
                        for use in (dil > 1, dil > FAST_STRIDE)],
        compiler_params=_params(("parallel",)),
        name=f"qkv_proj_g{g}",
    )(*args)


def _gate_body(*refs, tiles_per_seq, n_riders):
    h_ref, wu_ref, wb_ref, wc_ref, wa_ref, wv_ref, cw_ref = refs[:7]
    rider_src = refs[7:7 + n_riders]
    y_ref, sa_ref, sv_ref = refs[7 + n_riders:10 + n_riders]
    rider_dst = refs[10 + n_riders:10 + 2 * n_riders]
    tail_ref = refs[10 + 2 * n_riders]
    i, j = pl.program_id(0), pl.program_id(1)

    @pl.when(i == 0)
    def _():
        tail_ref[j] = jnp.zeros(tail_ref.shape[1:], F32)

    tail = jnp.where(i % tiles_per_seq == 0, 0.0, tail_ref[j])
    rows = lax.broadcasted_iota(jnp.int32, (GATE_SUB, GATE_TN), 0)
    for ch in range(GATE_TM // GATE_SUB):
        tok = pl.ds(ch * GATE_SUB, GATE_SUB)
        h = h_ref[tok, :]
        dot = lambda w_ref: jnp.dot(h, w_ref[...], preferred_element_type=F32)
        z = dot(wc_ref) * dot(wu_ref)
        p1 = tail[SUBLANES - 1:SUBLANES, :]
        p2 = tail[SUBLANES - 2:SUBLANES - 1, :]
        zm1 = jnp.where(rows == 0, p1, pltpu.roll(z, 1, 0))
        zm2 = jnp.where(rows == 0, p2, jnp.where(rows == 1, p1, pltpu.roll(z, 2, 0)))
        conv = cw_ref[2:3, :] * z + cw_ref[1:2, :] * zm1 + cw_ref[0:1, :] * zm2
        tail = z[GATE_SUB - SUBLANES:, :]
        y_ref[tok, :] = (dot(wb_ref) * conv).astype(BF16)
        sa_ref[tok, :] = jax.nn.sigmoid(dot(wa_ref)).astype(BF16)
        sv_ref[tok, :] = jax.nn.sigmoid(dot(wv_ref)).astype(BF16)
    tail_ref[j] = tail
    _cast_riders(rider_src, rider_dst)


def _gate_proj(h, w_in, conv_w, seq, riders=()):
    t, d = h.shape
    assert seq % GATE_TM == 0
    grid = (t // GATE_TM, d // GATE_TN)
    wspec = lambda off: pl.BlockSpec((d, GATE_TN), lambda i, j: (0, off // GATE_TN + j))
    ospec = pl.BlockSpec((GATE_TM, GATE_TN), lambda i, j: (i, j))
    oshape = jax.ShapeDtypeStruct((t, d), BF16)
    rider_specs = [_rider_spec(w, grid) for w in riders]
    out = pl.pallas_call(
        functools.partial(_gate_body, tiles_per_seq=seq // GATE_TM, n_riders=len(riders)),
        grid=grid,
        in_specs=[pl.BlockSpec((GATE_TM, d), lambda i, j: (i, 0)),
                  wspec(U_OFF), wspec(GB_OFF), wspec(GC_OFF), wspec(GA_OFF), wspec(GV_OFF),
                  pl.BlockSpec((CONV_WIDTH, GATE_TN), lambda i, j: (0, j))] + rider_specs,
        out_specs=[ospec] * 3 + rider_specs,
        out_shape=[oshape] * 3 + _rider_shapes(riders),
        scratch_shapes=[pltpu.VMEM((d // GATE_TN, SUBLANES, GATE_TN), F32)],
        compiler_params=_params(("arbitrary", "arbitrary")),
        name="gate_proj",
    )(h, w_in, w_in, w_in, w_in, w_in, conv_w, *riders)
    return out[:3], out[3:]


def _attn_body(q_ref, kp_ref, kc_ref, vp_ref, vc_ref, o_ref, lse_ref, kbuf, vbuf, *, slopes, dilation, rows):
    c = pl.program_id(1)
    n_sub = rows // BLOCK
    kbuf[:, :BLOCK, :] = kp_ref[0, 0]
    kbuf[:, BLOCK:, :] = kc_ref[0, 0]
    vbuf[:, :BLOCK, :] = vp_ref[0, 0]
    vbuf[:, BLOCK:, :] = vc_ref[0, 0]

    qi = lax.broadcasted_iota(jnp.int32, (BLOCK, 2 * BLOCK), 0)
    ki = lax.broadcasted_iota(jnp.int32, (BLOCK, 2 * BLOCK), 1)
    dist = BLOCK + qi - ki
    band = (dist >= 0) & (dist <= BLOCK)
    in_cur = ki >= BLOCK
    ddist = (dilation * dist).astype(F32)
    lane = lax.broadcasted_iota(jnp.int32, (BLOCK, HEAD_DIM), 1)

    def sub_block(it, carry):
        r = it // n_sub
        sb = it % n_sub
        row0 = pl.multiple_of(sb * BLOCK, BLOCK)
        valid = band & (in_cur | (sb > 0) | (c > 0))
        if dilation == 1:
            out_rows = pl.ds(row0, BLOCK)
        else:
            out_rows = pl.ds(row0 * dilation + r, BLOCK, stride=dilation)
        lse = None
        for hh in range(HEADS_PER_GROUP):
            q = q_ref[0, 0, r, pl.ds(row0, BLOCK), _head(hh)]
            k = kbuf[r, pl.ds(row0, 2 * BLOCK), _head(hh)]
            v = vbuf[r, pl.ds(row0, 2 * BLOCK), _head(hh)]
            s = lax.dot_general(q, k, (((1,), (1,)), ((), ())), preferred_element_type=F32)
            s = jnp.where(valid, s - slopes[hh] * ddist, NEG_INF)
            m = jnp.max(s, axis=-1, keepdims=True)
            p = jnp.exp(s - m)
            l = jnp.sum(p, axis=-1, keepdims=True)
            o_ref[hh, out_rows, :] = jnp.dot(p.astype(BF16), v, preferred_element_type=F32) / l
            lse_h = jnp.broadcast_to(m + jnp.log(l), (BLOCK, HEAD_DIM))
            lse = lse_h if lse is None else jnp.where(lane >= hh * LSE_LANES, lse_h, lse)
        lse_ref[out_rows, :] = lse
        return carry

    lax.fori_loop(0, dilation * n_sub, sub_block, 0, unroll=8)


def _attention_group(qkv, g, batch, seq):
    window, dil = DILATED_GROUPS[g]
    assert window // dil == BLOCK and seq % ATTN_CHUNK == 0 and ATTN_CHUNK % (dil * BLOCK) == 0
    rows = ATTN_CHUNK // dil
    chunks = seq // ATTN_CHUNK
    cur = lambda part: pl.BlockSpec((1, 1, dil, rows, GROUP_WIDTH), lambda b, c: (part, b, 0, c, 0))
    prev = lambda part: pl.BlockSpec((1, 1, dil, BLOCK, GROUP_WIDTH),
                                     lambda b, c: (part, b, 0, jnp.maximum(c * (rows // BLOCK) - 1, 0), 0))
    ospec = pl.BlockSpec((HEADS_PER_GROUP, ATTN_CHUNK, HEAD_DIM), lambda b, c: (0, b * chunks + c, 0))
    oshape = jax.ShapeDtypeStruct((HEADS_PER_GROUP, batch * seq, HEAD_DIM), F32)
    lspec = pl.BlockSpec((ATTN_CHUNK, HEAD_DIM), lambda b, c: (b * chunks + c, 0))
    lshape = jax.ShapeDtypeStruct((batch * seq, HEAD_DIM), F32)
    slopes = ALIBI_SLOPES[g * HEADS_PER_GROUP:(g + 1) * HEADS_PER_GROUP]
    return pl.pallas_call(
        functools.partial(_attn_body, slopes=slopes, dilation=dil, rows=rows),
        grid=(batch, chunks),
        in_specs=[cur(0), prev(1), cur(1), prev(2), cur(2)],
        out_specs=(ospec, lspec),
        out_shape=(oshape, lshape),
        scratch_shapes=[pltpu.VMEM((dil, BLOCK + rows, GROUP_WIDTH), BF16)] * 2,
        compiler_params=_params(("parallel", "arbitrary")),
        name=f"attention_g{g}",
    )(qkv, qkv, qkv, qkv, qkv)


def _mix_body(o0_ref, o1_ref, o2_ref, l0_ref, l1_ref, l2_ref, y_ref, sa_ref, sv_ref, x_ref,
              wa_ref, wc_ref, wo_ref, out_ref):
    branch_b = jnp.dot(y_ref[...], wc_ref[...], preferred_element_type=F32)

    l0, l1, l2 = l0_ref[...], l1_ref[...], l2_ref[...]
    m = jnp.maximum(jnp.maximum(l0, l1), l2)
    e0, e1, e2 = jnp.exp(l0 - m), jnp.exp(l1 - m), jnp.exp(l2 - m)
    inv = 1.0 / (e0 + e1 + e2)
    alphas = (e0 * inv, e1 * inv, e2 * inv)
    heads = []
    for hh in range(HEADS_PER_GROUP):
        a0, a1, a2 = (jnp.broadcast_to(a[:, hh * LSE_LANES:hh * LSE_LANES + 1], (MIX_TM, HEAD_DIM)) for a in alphas)
        heads.append((a0 * o0_ref[hh] + a1 * o1_ref[hh] + a2 * o2_ref[hh]).astype(BF16))
    attn = jnp.concatenate(heads, axis=-1)
    branch_a = jnp.dot(attn, wa_ref[...], preferred_element_type=F32)

    merged = sa_ref[...].astype(F32) * branch_a + sv_ref[...].astype(F32) * branch_b
    out_ref[...] = x_ref[...] + jnp.dot(merged.astype(BF16), wo_ref[...], preferred_element_type=F32)


def _mix(os, lses, y, sa, sv, x, wa, wc, wo):
    t, d = x.shape
    row = pl.BlockSpec((MIX_TM, d), lambda i: (i, 0))
    head_rows = pl.BlockSpec((HEADS_PER_GROUP, MIX_TM, HEAD_DIM), lambda i: (0, i, 0))
    lse_rows = pl.BlockSpec((MIX_TM, HEAD_DIM), lambda i: (i, 0))
    const = lambda a: pl.BlockSpec(a.shape, lambda i: (0, 0), pipeline_mode=pl.Buffered(1))
    return pl.pallas_call(
        _mix_body,
        grid=(t // MIX_TM,),
        in_specs=[head_rows] * 3 + [lse_rows] * 3 + [row, row, row, row, const(wa), const(wc), const(wo)],
        out_specs=row,
        out_shape=jax.ShapeDtypeStruct((t, d), F32),
        compiler_params=_params(("parallel",)),
        name="mix",
    )(*os, *lses, y, sa, sv, x, wa, wc, wo)


def kernel(x, ffn1_norm, ffn1_w_gate, ffn1_w_up, ffn1_w_down, mix_norm, w_in, q_norm, k_norm, conv_w,
           w_attn_out, w_conv_out, w_o, ffn2_norm, ffn2_w_gate, ffn2_w_up, ffn2_w_down):
    batch, seq, d = x.shape
    t = batch * seq
    xt = x.reshape(t, d)
    for layer in range(ffn1_norm.shape[0]):
        bf = lambda w: w[layer].astype(BF16)
        xt, (w_in_b,) = _ffn(xt, ffn1_norm[layer], bf(ffn1_w_gate), bf(ffn1_w_up), bf(ffn1_w_down),
                             riders=[w_in[layer]])
        qk_gain = jnp.concatenate([q_norm[layer].reshape(-1) * (HEAD_DIM ** -0.5),
                                   k_norm[layer].reshape(-1)]).reshape(1, 2 * ATTN_WIDTH)
        os, lses = [], []
        h = None
        for g in range(N_GROUPS):
            w_g = jnp.concatenate([w_in_b[:, p * ATTN_WIDTH + g * GROUP_WIDTH:p * ATTN_WIDTH + (g + 1) * GROUP_WIDTH]
                                   for p in range(3)], axis=1)
            if h is None:
                qkv, h = _qkv_proj(xt, w_g, qk_gain, g, batch, seq, pre_norm_gain=mix_norm[layer])
            else:
                qkv = _qkv_proj(h, w_g, qk_gain, g, batch, seq)
            o_g, lse_g = _attention_group(qkv, g, batch, seq)
            os.append(o_g)
            lses.append(lse_g)
        later = (w_attn_out, w_conv_out, w_o, ffn2_w_gate, ffn2_w_up, ffn2_w_down)
        (y, sa, sv), (wa_b, wc_b, wo_b, wg2_b, wu2_b, wd2_b) = _gate_proj(
            h, w_in_b, conv_w[layer], seq, riders=[w[layer] for w in later])
        xt = _mix(os, lses, y, sa, sv, xt, wa_b, wc_b, wo_b)
        xt, _ = _ffn(xt, ffn2_norm[layer], wg2_b, wu2_b, wd2_b)
    return xt.reshape(batch, seq, d)
```

```python
import functools

import jax
import jax.numpy as jnp
from jax import lax
from jax.experimental import pallas as pl
from jax.experimental.pallas import tpu as pltpu

F32 = jnp.float32
BF16 = jnp.bfloat16

D_MODEL = 2048
D_FF = 5632
HEAD_DIM = 128
DILATED_GROUPS = ((128, 1), (512, 4), (2048, 16))
N_GROUPS = len(DILATED_GROUPS)
HEADS_PER_GROUP = 4
N_ATTN_HEADS = N_GROUPS * HEADS_PER_GROUP
ATTN_WIDTH = N_ATTN_HEADS * HEAD_DIM
GROUP_WIDTH = HEADS_PER_GROUP * HEAD_DIM
QKV_WIDTH = 3 * ATTN_WIDTH
CONV_WIDTH = 3
BLOCK = 128
LSE_LANES = HEAD_DIM // HEADS_PER_GROUP
RMS_EPS = 1e-6
NEG_INF = -1e30
ALIBI_SLOPES = tuple(2.0 ** (-8.0 * (i + 1) / N_ATTN_HEADS) for i in range(N_ATTN_HEADS))
U_OFF = QKV_WIDTH
GB_OFF = U_OFF + D_MODEL
GC_OFF = GB_OFF + D_MODEL
GA_OFF = GC_OFF + D_MODEL
GV_OFF = GA_OFF + D_MODEL

VMEM_LIMIT_BYTES = 60 * 1024 * 1024
SUBLANES = 8
FAST_STRIDE = 4
BF16_SUBLANES = 16

FFN_TM, FFN_TF = 1024, 512
QKV_TM, QKV_SUB = 1024, 256
GATE_TM, GATE_TN, GATE_SUB = 1024, 512, 128
ATTN_CHUNK = 2048
MIX_TM = 512


def _params(sem):
    return pltpu.CompilerParams(dimension_semantics=sem, vmem_limit_bytes=VMEM_LIMIT_BYTES)


def _rms(x, gain):
    ms = jnp.mean(x * x, axis=-1, keepdims=True)
    return x * lax.rsqrt(ms + RMS_EPS) * gain


def _head(hh):
    return slice(hh * HEAD_DIM, (hh + 1) * HEAD_DIM)


def _rider_spec(w, grid):
    rows, cols = w.shape
    steps = grid[0] * grid[1]
    rb = next(r for r in range(BF16_SUBLANES, rows + 1, BF16_SUBLANES) if rows % r == 0 and rows // r <= steps)
    n_blocks = rows // rb
    repeat = steps // n_blocks
    return pl.BlockSpec((rb, cols), lambda i, j: (jnp.minimum((i * grid[1] + j) // repeat, n_blocks - 1), 0))


def _cast_riders(srcs, dsts):
    for src, dst in zip(srcs, dsts):
        dst[...] = src[...].astype(BF16)


def _rider_shapes(riders):
    return [jax.ShapeDtypeStruct(w.shape, BF16) for w in riders]


def _ffn_body(*refs, n_riders):
    x_ref, gain_ref, wg_ref, wu_ref, wd_ref = refs[:5]
    rider_src = refs[5:5 + n_riders]
    o_ref = refs[5 + n_riders]
    rider_dst = refs[6 + n_riders:6 + 2 * n_riders]
    xn_ref = refs[6 + 2 * n_riders]

    def chunk(xn, base):
        _cast_riders(rider_src, rider_dst)
        g = jnp.dot(xn, wg_ref[...], preferred_element_type=F32)
        u = jnp.dot(xn, wu_ref[...], preferred_element_type=F32)
        a = (g * jax.nn.sigmoid(g) * (0.5 * u)).astype(BF16)
        o_ref[...] = base + jnp.dot(a, wd_ref[...], preferred_element_type=F32)

    @pl.when(pl.program_id(1) == 0)
    def _():
        x = x_ref[...]
        xn = _rms(x, gain_ref[...]).astype(BF16)
        xn_ref[...] = xn
        chunk(xn, x)

    @pl.when(pl.program_id(1) > 0)
    def _():
        chunk(xn_ref[...], o_ref[...])


def _ffn(x, gain, wg, wu, wd, riders=()):
    t, d = x.shape
    grid = (t // FFN_TM, D_FF // FFN_TF)
    row = pl.BlockSpec((FFN_TM, d), lambda i, j: (i, 0))
    rider_specs = [_rider_spec(w, grid) for w in riders]
    out = pl.pallas_call(
        functools.partial(_ffn_body, n_riders=len(riders)),
        grid=grid,
        in_specs=[row,
                  pl.BlockSpec((1, d), lambda i, j: (0, 0)),
                  pl.BlockSpec((d, FFN_TF), lambda i, j: (0, j)),
                  pl.BlockSpec((d, FFN_TF), lambda i, j: (0, j)),
                  pl.BlockSpec((FFN_TF, d), lambda i, j: (j, 0))] + rider_specs,
        out_specs=[row] + rider_specs,
        out_shape=[jax.ShapeDtypeStruct((t, d), F32)] + _rider_shapes(riders),
        scratch_shapes=[pltpu.VMEM((FFN_TM, d), BF16)],
        compiler_params=_params(("arbitrary", "arbitrary")),
        name="ffn",
    )(x, gain.reshape(1, d), wg, wu, wd, *riders)
    return out[0], out[1:]


def _qkv_body(*refs, dilation, from_x):
    if from_x:
        x_ref, ng_ref, w_ref, qg_ref, kg_ref, o_ref, h_ref, scr_ref, tmp_ref = refs
    else:
        h_ref, w_ref, qg_ref, kg_ref, o_ref, scr_ref, tmp_ref = refs
    w = w_ref[...]
    gains = (qg_ref, kg_ref)
    sub = QKV_SUB // dilation
    for ch in range(QKV_TM // QKV_SUB):
        tok = pl.ds(ch * QKV_SUB, QKV_SUB)
        if from_x:
            h = _rms(x_ref[tok, :], ng_ref[...]).astype(BF16)
            h_ref[tok, :] = h
        else:
            h = h_ref[tok, :]
        acc = jnp.dot(h, w, preferred_element_type=F32)
        for part in range(3):
            for hh in range(HEADS_PER_GROUP):
                val = acc[:, part * GROUP_WIDTH + hh * HEAD_DIM:part * GROUP_WIDTH + (hh + 1) * HEAD_DIM]
                if part < 2:
                    val = _rms(val, gains[part][:, _head(hh)])
                if dilation == 1:
                    o_ref[part, 0, 0, tok, _head(hh)] = val.astype(BF16)
                else:
                    scr_ref[part * HEADS_PER_GROUP + hh, tok, :] = val
        if 1 < dilation <= FAST_STRIDE:
            for slab in range(3 * HEADS_PER_GROUP):
                part, hh = divmod(slab, HEADS_PER_GROUP)
                for r in range(dilation):
                    src = scr_ref[slab, pl.ds(ch * QKV_SUB + r, sub, stride=dilation), :]
                    o_ref[part, 0, r, pl.ds(ch * sub, sub), _head(hh)] = src.astype(BF16)
        elif dilation > FAST_STRIDE:
            outer = dilation // FAST_STRIDE
            quarter = QKV_SUB // FAST_STRIDE
            for slab in range(3 * HEADS_PER_GROUP):
                part, hh = divmod(slab, HEADS_PER_GROUP)
                for a in range(FAST_STRIDE):
                    tmp_ref[slab, pl.ds(ch * QKV_SUB + a * quarter, quarter), :] = (
                        scr_ref[slab, pl.ds(ch * QKV_SUB + a, quarter, stride=FAST_STRIDE), :])
                for a in range(FAST_STRIDE):
                    for q in range(outer):
                        src = tmp_ref[slab, pl.ds(ch * QKV_SUB + a * quarter + q, sub, stride=outer), :]
                        o_ref[part, 0, a + FAST_STRIDE * q, pl.ds(ch * sub, sub), _head(hh)] = src.astype(BF16)


def _qkv_proj(h, w_g, qk_gain, g, batch, seq, pre_norm_gain=None):
    _, dil = DILATED_GROUPS[g]
    t, d = h.shape
    tiles_per_seq = seq // QKV_TM
    rows = QKV_TM // dil
    from_x = pre_norm_gain is not None
    row = pl.BlockSpec((QKV_TM, d), lambda i: (i, 0))
    in_specs = [row]
    args = [h]
    if from_x:
        in_specs.append(pl.BlockSpec((1, d), lambda i: (0, 0)))
        args.append(pre_norm_gain.reshape(1, d))
    in_specs += [pl.BlockSpec((d, 3 * GROUP_WIDTH), lambda i: (0, 0), pipeline_mode=pl.Buffered(1)),
                 pl.BlockSpec((1, GROUP_WIDTH), lambda i: (0, g)),
                 pl.BlockSpec((1, GROUP_WIDTH), lambda i: (0, N_GROUPS + g))]
    args += [w_g, qk_gain, qk_gain]
    out_specs = pl.BlockSpec((3, 1, dil, rows, GROUP_WIDTH),
                             lambda i: (0, i // tiles_per_seq, 0, i % tiles_per_seq, 0))
    out_shape = jax.ShapeDtypeStruct((3, batch, dil, seq // dil, GROUP_WIDTH), BF16)
    if from_x:
        out_specs = (out_specs, row)
        out_shape = (out_shape, jax.ShapeDtypeStruct((t, d), BF16))
    return pl.pallas_call(
        functools.partial(_qkv_body, dilation=dil, from_x=from_x),
        grid=(t // QKV_TM,),
        in_specs=in_specs,
        out_specs=out_specs,
        out_shape=out_shape,
        scratch_shapes=[pltpu.VMEM((3 * HEADS_PER_GROUP, QKV_TM if use else SUBLANES, HEAD_DIM), F32)
                        for use in (dil > 1, dil > FAST_STRIDE)],
        compiler_params=_params(("parallel",)),
        name=f"qkv_proj_g{g}",
    )(*args)


def _gate_body(*refs, tiles_per_seq, n_riders):
    h_ref, wu_ref, wb_ref, wc_ref, wa_ref, wv_ref, cw_ref = refs[:7]
    rider_src = refs[7:7 + n_riders]
    y_ref, sa_ref, sv_ref = refs[7 + n_riders:10 + n_riders]
    rider_dst = refs[10 + n_riders:10 + 2 * n_riders]
    tail_ref = refs[10 + 2 * n_riders]
    i, j = pl.program_id(0), pl.program_id(1)

    @pl.when(i == 0)
    def _():
        tail_ref[j] = jnp.zeros(tail_ref.shape[1:], F32)

    tail = jnp.where(i % tiles_per_seq == 0, 0.0, tail_ref[j])
    rows = lax.broadcasted_iota(jnp.int32, (GATE_SUB, GATE_TN), 0)
    for ch in range(GATE_TM // GATE_SUB):
        tok = pl.ds(ch * GATE_SUB, GATE_SUB)
        h = h_ref[tok, :]
        dot = lambda w_ref: jnp.dot(h, w_ref[...], preferred_element_type=F32)
        z = dot(wc_ref) * dot(wu_ref)
        p1 = tail[SUBLANES - 1:SUBLANES, :]
        p2 = tail[SUBLANES - 2:SUBLANES - 1, :]
        zm1 = jnp.where(rows == 0, p1, pltpu.roll(z, 1, 0))
        zm2 = jnp.where(rows == 0, p2, jnp.where(rows == 1, p1, pltpu.roll(z, 2, 0)))
        conv = cw_ref[2:3, :] * z + cw_ref[1:2, :] * zm1 + cw_ref[0:1, :] * zm2
        tail = z[GATE_SUB - SUBLANES:, :]
        y_ref[tok, :] = (dot(wb_ref) * conv).astype(BF16)
        sa_ref[tok, :] = jax.nn.sigmoid(dot(wa_ref)).astype(BF16)
        sv_ref[tok, :] = jax.nn.sigmoid(dot(wv_ref)).astype(BF16)
    tail_ref[j] = tail
    _cast_riders(rider_src, rider_dst)


def _gate_proj(h, w_in, conv_w, seq, riders=()):
    t, d = h.shape
    assert seq % GATE_TM == 0
    grid = (t // GATE_TM, d // GATE_TN)
    wspec = lambda off: pl.BlockSpec((d, GATE_TN), lambda i, j: (0, off // GATE_TN + j))
    ospec = pl.BlockSpec((GATE_TM, GATE_TN), lambda i, j: (i, j))
    oshape = jax.ShapeDtypeStruct((t, d), BF16)
    rider_specs = [_rider_spec(w, grid) for w in riders]
    out = pl.pallas_call(
        functools.partial(_gate_body, tiles_per_seq=seq // GATE_TM, n_riders=len(riders)),
        grid=grid,
        in_specs=[pl.BlockSpec((GATE_TM, d), lambda i, j: (i, 0)),
                  wspec(U_OFF), wspec(GB_OFF), wspec(GC_OFF), wspec(GA_OFF), wspec(GV_OFF),
                  pl.BlockSpec((CONV_WIDTH, GATE_TN), lambda i, j: (0, j))] + rider_specs,
        out_specs=[ospec] * 3 + rider_specs,
        out_shape=[oshape] * 3 + _rider_shapes(riders),
        scratch_shapes=[pltpu.VMEM((d // GATE_TN, SUBLANES, GATE_TN), F32)],
        compiler_params=_params(("arbitrary", "arbitrary")),
        name="gate_proj",
    )(h, w_in, w_in, w_in, w_in, w_in, conv_w, *riders)
    return out[:3], out[3:]


def _attn_body(q_ref, kp_ref, kc_ref, vp_ref, vc_ref, o_ref, lse_ref, *, slopes, dilation, rows):
    c = pl.program_id(1)
    qi = lax.broadcasted_iota(jnp.int32, (BLOCK, 2 * BLOCK), 0)
    ki = lax.broadcasted_iota(jnp.int32, (BLOCK, 2 * BLOCK), 1)
    dist = BLOCK + qi - ki
    band = (dist >= 0) & (dist <= BLOCK)
    band_first = band & ((ki >= BLOCK) | (c > 0))
    ddist = (dilation * dist).astype(F32)
    lane = lax.broadcasted_iota(jnp.int32, (BLOCK, HEAD_DIM), 1)

    def window(prev_ref, cur_ref, r, sb, hh):
        if sb == 0:
            return jnp.concatenate([prev_ref[0, 0, r, :, _head(hh)], cur_ref[0, 0, r, :BLOCK, _head(hh)]], axis=0)
        return cur_ref[0, 0, r, (sb - 1) * BLOCK:(sb + 1) * BLOCK, _head(hh)]

    for r in range(dilation):
        for sb in range(rows // BLOCK):
            valid = band_first if sb == 0 else band
            if dilation == 1:
                out_rows = pl.ds(sb * BLOCK, BLOCK)
            else:
                out_rows = pl.ds(sb * BLOCK * dilation + r, BLOCK, stride=dilation)
            lse = None
            for hh in range(HEADS_PER_GROUP):
                q = q_ref[0, 0, r, sb * BLOCK:(sb + 1) * BLOCK, _head(hh)]
                k = window(kp_ref, kc_ref, r, sb, hh)
                v = window(vp_ref, vc_ref, r, sb, hh)
                s = lax.dot_general(q, k, (((1,), (1,)), ((), ())), preferred_element_type=F32)
                s = jnp.where(valid, s - slopes[hh] * ddist, NEG_INF)
                m = jnp.max(s, axis=-1, keepdims=True)
                p = jnp.exp(s - m)
                l = jnp.sum(p, axis=-1, keepdims=True)
                o_ref[hh, out_rows, :] = jnp.dot(p.astype(BF16), v, preferred_element_type=F32) / l
                lse_h = jnp.broadcast_to(m + jnp.log(l), (BLOCK, HEAD_DIM))
                lse = lse_h if lse is None else jnp.where(lane >= hh * LSE_LANES, lse_h, lse)
            lse_ref[out_rows, :] = lse


def _attention_group(qkv, g, batch, seq):
    window, dil = DILATED_GROUPS[g]
    assert window // dil == BLOCK and seq % ATTN_CHUNK == 0 and ATTN_CHUNK % (dil * BLOCK) == 0
    rows = ATTN_CHUNK // dil
    chunks = seq // ATTN_CHUNK
    cur = lambda part: pl.BlockSpec((1, 1, dil, rows, GROUP_WIDTH), lambda b, c: (part, b, 0, c, 0))
    prev = lambda part: pl.BlockSpec((1, 1, dil, BLOCK, GROUP_WIDTH),
                                     lambda b, c: (part, b, 0, jnp.maximum(c * (rows // BLOCK) - 1, 0), 0))
    ospec = pl.BlockSpec((HEADS_PER_GROUP, ATTN_CHUNK, HEAD_DIM), lambda b, c: (0, b * chunks + c, 0))
    oshape = jax.ShapeDtypeStruct((HEADS_PER_GROUP, batch * seq, HEAD_DIM), F32)
    lspec = pl.BlockSpec((ATTN_CHUNK, HEAD_DIM), lambda b, c: (b * chunks + c, 0))
    lshape = jax.ShapeDtypeStruct((batch * seq, HEAD_DIM), F32)
    slopes = ALIBI_SLOPES[g * HEADS_PER_GROUP:(g + 1) * HEADS_PER_GROUP]
    return pl.pallas_call(
        functools.partial(_attn_body, slopes=slopes, dilation=dil, rows=rows),
        grid=(batch, chunks),
        in_specs=[cur(0), prev(1), cur(1), prev(2), cur(2)],
        out_specs=(ospec, lspec),
        out_shape=(oshape, lshape),
        compiler_params=_params(("parallel", "arbitrary")),
        name=f"attention_g{g}",
    )(qkv, qkv, qkv, qkv, qkv)


def _mix_body(o0_ref, o1_ref, o2_ref, l0_ref, l1_ref, l2_ref, y_ref, sa_ref, sv_ref, x_ref,
              wa_ref, wc_ref, wo_ref, out_ref):
    branch_b = jnp.dot(y_ref[...], wc_ref[...], preferred_element_type=F32)

    l0, l1, l2 = l0_ref[...], l1_ref[...], l2_ref[...]
    m = jnp.maximum(jnp.maximum(l0, l1), l2)
    e0, e1, e2 = jnp.exp(l0 - m), jnp.exp(l1 - m), jnp.exp(l2 - m)
    inv = 1.0 / (e0 + e1 + e2)
    alphas = (e0 * inv, e1 * inv, e2 * inv)
    heads = []
    for hh in range(HEADS_PER_GROUP):
        a0, a1, a2 = (jnp.broadcast_to(a[:, hh * LSE_LANES:hh * LSE_LANES + 1], (MIX_TM, HEAD_DIM)) for a in alphas)
        heads.append((a0 * o0_ref[hh] + a1 * o1_ref[hh] + a2 * o2_ref[hh]).astype(BF16))
    attn = jnp.concatenate(heads, axis=-1)
    branch_a = jnp.dot(attn, wa_ref[...], preferred_element_type=F32)

    merged = sa_ref[...].astype(F32) * branch_a + sv_ref[...].astype(F32) * branch_b
    out_ref[...] = x_ref[...] + jnp.dot(merged.astype(BF16), wo_ref[...], preferred_element_type=F32)


def _mix(os, lses, y, sa, sv, x, wa, wc, wo):
    t, d = x.shape
    row = pl.BlockSpec((MIX_TM, d), lambda i: (i, 0))
    head_rows = pl.BlockSpec((HEADS_PER_GROUP, MIX_TM, HEAD_DIM), lambda i: (0, i, 0))
    lse_rows = pl.BlockSpec((MIX_TM, HEAD_DIM), lambda i: (i, 0))
    const = lambda a: pl.BlockSpec(a.shape, lambda i: (0, 0), pipeline_mode=pl.Buffered(1))
    return pl.pallas_call(
        _mix_body,
        grid=(t // MIX_TM,),
        in_specs=[head_rows] * 3 + [lse_rows] * 3 + [row, row, row, row, const(wa), const(wc), const(wo)],
        out_specs=row,
        out_shape=jax.ShapeDtypeStruct((t, d), F32),
        compiler_params=_params(("parallel",)),
        name="mix",
    )(*os, *lses, y, sa, sv, x, wa, wc, wo)


def kernel(x, ffn1_norm, ffn1_w_gate, ffn1_w_up, ffn1_w_down, mix_norm, w_in, q_norm, k_norm, conv_w,
           w_attn_out, w_conv_out, w_o, ffn2_norm, ffn2_w_gate, ffn2_w_up, ffn2_w_down):
    batch, seq, d = x.shape
    t = batch * seq
    xt = x.reshape(t, d)
    for layer in range(ffn1_norm.shape[0]):
        bf = lambda w: w[layer].astype(BF16)
        xt, (w_in_b,) = _ffn(xt, ffn1_norm[layer], bf(ffn1_w_gate), bf(ffn1_w_up), bf(ffn1_w_down),
                             riders=[w_in[layer]])
        qk_gain = jnp.concatenate([q_norm[layer].reshape(-1) * (HEAD_DIM ** -0.5),
                                   k_norm[layer].reshape(-1)]).reshape(1, 2 * ATTN_WIDTH)
        os, lses = [], []
        h = None
        for g in range(N_GROUPS):
            w_g = jnp.concatenate([w_in_b[:, p * ATTN_WIDTH + g * GROUP_WIDTH:p * ATTN_WIDTH + (g + 1) * GROUP_WIDTH]
                                   for p in range(3)], axis=1)
            if h is None:
                qkv, h = _qkv_proj(xt, w_g, qk_gain, g, batch, seq, pre_norm_gain=mix_norm[layer])
            else:
                qkv = _qkv_proj(h, w_g, qk_gain, g, batch, seq)
            o_g, lse_g = _attention_group(qkv, g, batch, seq)
            os.append(o_g)
            lses.append(lse_g)
        later = (w_attn_out, w_conv_out, w_o, ffn2_w_gate, ffn2_w_up, ffn2_w_down)
        (y, sa, sv), (wa_b, wc_b, wo_b, wg2_b, wu2_b, wd2_b) = _gate_proj(
            h, w_in_b, conv_w[layer], seq, riders=[w[layer] for w in later])
        xt = _mix(os, lses, y, sa, sv, xt, wa_b, wc_b, wo_b)
        xt, _ = _ffn(xt, ffn2_norm[layer], wg2_b, wu2_b, wd2_b)
    return xt.reshape(batch, seq, d)
```
